```python
import jax, jax.numpy as jnp
from jax import lax
import numpy as np

D_MODEL = 2048
BATCH = 2
SEQ = 16384
DEPTH = 2

D_MIX = D_MODEL
POOL_WINDOWS = (2, 4, 8, 16)
POOL_WIDTH = D_MIX // 2
POOL_GROUP = POOL_WIDTH // len(POOL_WINDOWS)
N_HEADS = 8
HEAD_DIM = 128
N_KV_GROUPS = 2
HEADS_PER_GROUP = N_HEADS // N_KV_GROUPS
NSA_WIDTH = N_HEADS * HEAD_DIM
KV_WIDTH = N_KV_GROUPS * HEAD_DIM
CMP_BLOCK = 32
CMP_STRIDE = 16
CMP_HIDDEN = 512
SEL_BLOCK = 64
SEL_RATIO = SEL_BLOCK // CMP_STRIDE
N_SELECT = 16
WINDOW = 512
Q_BLOCK = 128
IN_WIDTH = POOL_WIDTH + NSA_WIDTH + 6 * KV_WIDTH + 3 * N_HEADS
N_EXPERTS = 32
TOP_K = 4
D_EXPERT = D_MODEL
SWIGLU_ALPHA = 1.702
SWIGLU_LIMIT = 7.0
MOE_BLOCK = 128
NORM_EPS = 1e-5
NEG = -1e30
BIG = 1e30

kernel_name = 'hybrid_pool_nsa_moe_block'


def rmsnorm(x, g):
    x32 = x.astype(jnp.float32)
    y = x32 * lax.rsqrt(jnp.mean(x32 * x32, axis=-1, keepdims=True) + NORM_EPS)
    return (y * g.astype(jnp.float32)).astype(x.dtype)


def masked_softmax(s, mask):
    s = jnp.where(mask, s.astype(jnp.float32), NEG)
    return jax.nn.softmax(s, axis=-1) * mask


def pool_mixer(u, pool_w, pool_scale):
    s = u.shape[1]
    t = jnp.arange(s)
    outs = []
    for gi, w in enumerate(POOL_WINDOWS):
        ug = u[..., gi * POOL_GROUP:(gi + 1) * POOL_GROUP]
        cs = jnp.cumsum(ug.astype(jnp.float32), axis=1)
        lag = jnp.pad(cs, ((0, 0), (w, 0), (0, 0)))[:, :s]
        cnt = jnp.minimum(t + 1, w).astype(jnp.float32)[None, :, None]
        mean = ((cs - lag) / cnt).astype(u.dtype)
        outs.append((mean - ug) @ pool_w[gi])
    return jnp.concatenate(outs, axis=-1) * pool_scale


def compress_kv(k, pos, w1, b1, w2, b2):
    b, s, g, d = k.shape
    ch = k.reshape(b, s // CMP_STRIDE, CMP_STRIDE, g, d)
    blk = jnp.concatenate([ch[:, :-1], ch[:, 1:]], axis=2) + pos[None, None, :, None, :]
    nb = blk.shape[1]
    blk = blk.transpose(0, 1, 3, 2, 4).reshape(b, nb, g, CMP_BLOCK * d)
    return jax.nn.gelu(blk @ w1 + b1) @ w2 + b2


def nsa_mixer(q, k_cmp, v_cmp, k_slc, v_slc, k_win, v_win, gate_logits, cmp_k, cmp_v):
    b, s, _, d = q.shape
    dt = q.dtype
    scale = d ** -0.5
    kc = compress_kv(k_cmp, *cmp_k)
    vc = compress_kv(v_cmp, *cmp_v)
    nb = kc.shape[1]
    cmp_end = jnp.arange(nb) * CMP_STRIDE + (CMP_BLOCK - 1)
    ns = s // SEL_BLOCK
    n_sel = min(N_SELECT, ns)
    ks_blk = k_slc.reshape(b, ns, SEL_BLOCK, N_KV_GROUPS, d).transpose(0, 3, 1, 2, 4)
    vs_blk = v_slc.reshape(b, ns, SEL_BLOCK, N_KV_GROUPS, d).transpose(0, 3, 1, 2, 4)
    kw = jnp.pad(k_win, ((0, 0), (WINDOW, 0), (0, 0), (0, 0)))
    vw = jnp.pad(v_win, ((0, 0), (WINDOW, 0), (0, 0), (0, 0)))
    qg = q.reshape(b, s, N_KV_GROUPS, HEADS_PER_GROUP, d)
    gates = jax.nn.sigmoid(gate_logits.astype(jnp.float32)).astype(dt).reshape(b, s, N_KV_GROUPS, HEADS_PER_GROUP, 3)
    bi = jnp.arange(b)[:, None, None, None]
    gi = jnp.arange(N_KV_GROUPS)[None, :, None, None]
    j = jnp.arange(ns)

    def query_block(n):
        q0 = n * Q_BLOCK
        t = q0 + jnp.arange(Q_BLOCK)
        qb = lax.dynamic_slice_in_dim(qg, q0, Q_BLOCK, axis=1)
        gb = lax.dynamic_slice_in_dim(gates, q0, Q_BLOCK, axis=1)
        p_cmp = masked_softmax(jnp.einsum('bqgrd,bngd->bgrqn', qb, kc) * scale,
                               cmp_end[None, :] <= t[:, None])
        o_cmp = jnp.einsum('bgrqn,bngd->bqgrd', p_cmp.astype(dt), vc)
        imp = jnp.pad(p_cmp.sum(axis=2), ((0, 0), (0, 0), (0, 0), (0, ns * SEL_RATIO - nb)))
        imp = imp.reshape(b, N_KV_GROUPS, Q_BLOCK, ns, SEL_RATIO)
        imp = imp.sum(-1) + jnp.pad(imp[..., :-1, SEL_RATIO - 1], ((0, 0), (0, 0), (0, 0), (1, 0)))
        tj = (t // SEL_BLOCK)[:, None]
        forced = (j[None, :] == 0) | (j[None, :] == tj) | (j[None, :] == tj - 1)
        score = jnp.where(forced, BIG, jnp.where(j[None, :] * SEL_BLOCK <= t[:, None], imp, NEG))
        top_s, idx = lax.top_k(score, n_sel)
        kb = ks_blk[bi, gi, idx].reshape(b, N_KV_GROUPS, Q_BLOCK, n_sel * SEL_BLOCK, d)
        vb = vs_blk[bi, gi, idx].reshape(b, N_KV_GROUPS, Q_BLOCK, n_sel * SEL_BLOCK, d)
        pos = (idx[..., None] * SEL_BLOCK + jnp.arange(SEL_BLOCK)).reshape(b, N_KV_GROUPS, Q_BLOCK, n_sel * SEL_BLOCK)
        m_sel = (pos <= t[:, None]) & jnp.repeat(top_s >= 0, SEL_BLOCK, axis=-1)
        p_slc = masked_softmax(jnp.einsum('bqgrd,bgqkd->bgrqk', qb, kb) * scale, m_sel[:, :, None])
        o_slc = jnp.einsum('bgrqk,bgqkd->bqgrd', p_slc.astype(dt), vb)
        kwb = lax.dynamic_slice_in_dim(kw, q0, WINDOW + Q_BLOCK, axis=1)
        vwb = lax.dynamic_slice_in_dim(vw, q0, WINDOW + Q_BLOCK, axis=1)
        spos = q0 - WINDOW + jnp.arange(WINDOW + Q_BLOCK)
        dist = t[:, None] - spos[None, :]
        m_win = (dist >= 0) & (dist < WINDOW) & (spos[None, :] >= 0)
        p_win = masked_softmax(jnp.einsum('bqgrd,bkgd->bgrqk', qb, kwb) * scale, m_win)
        o_win = jnp.einsum('bgrqk,bkgd->bqgrd', p_win.astype(dt), vwb)
        return gb[..., 0:1] * o_cmp + gb[..., 1:2] * o_slc + gb[..., 2:3] * o_win

    out = lax.map(query_block, jnp.arange(s // Q_BLOCK))
    return out.transpose(1, 0, 2, 3, 4, 5).reshape(b, s, NSA_WIDTH)


def hybrid_mixer(h, w_in, pool_w, pool_scale, cmp_k, cmp_v, w_out):
    b, s, _ = h.shape
    u = h @ w_in
    u_pool = u[..., :POOL_WIDTH]
    q = u[..., POOL_WIDTH:POOL_WIDTH + NSA_WIDTH].reshape(b, s, N_HEADS, HEAD_DIM)
    kv0 = POOL_WIDTH + NSA_WIDTH
    kv = [u[..., kv0 + i * KV_WIDTH:kv0 + (i + 1) * KV_WIDTH].reshape(b, s, N_KV_GROUPS, HEAD_DIM) for i in range(6)]
    gate_logits = u[..., kv0 + 6 * KV_WIDTH:].reshape(b, s, N_HEADS, 3)
    y_pool = pool_mixer(u_pool, pool_w, pool_scale)
    y_nsa = nsa_mixer(q, kv[0], kv[1], kv[2], kv[3], kv[4], kv[5], gate_logits, cmp_k, cmp_v)
    return jnp.concatenate([y_pool, y_nsa], axis=-1) @ w_out


def clamped_swiglu(gl, lin):
    gl = jnp.minimum(gl, SWIGLU_LIMIT)
    lin = jnp.clip(lin, -SWIGLU_LIMIT, SWIGLU_LIMIT)
    return gl * jax.nn.sigmoid(SWIGLU_ALPHA * gl) * (lin + 1)


def moe(h, w_router, b_router, w_gate, b_gate, w_up, b_up, w_down, b_down):
    b, s, d = h.shape
    n_tok = b * s
    hf = h.reshape(n_tok, d)
    logits = (hf @ w_router + b_router).astype(jnp.float32)
    top_val, top_idx = lax.top_k(logits, TOP_K)
    weights = jax.nn.softmax(top_val, axis=-1).astype(h.dtype)
    n_assign = n_tok * TOP_K
    e_flat = top_idx.reshape(n_assign)
    tok_flat = jnp.arange(n_assign, dtype=jnp.int32) // TOP_K
    w_flat = weights.reshape(n_assign)
    order = jnp.argsort(e_flat)
    e_sorted = e_flat[order]
    counts = jnp.bincount(e_flat, length=N_EXPERTS)
    starts = jnp.cumsum(counts) - counts
    padded = (counts + MOE_BLOCK - 1) // MOE_BLOCK * MOE_BLOCK
    pends = jnp.cumsum(padded)
    pstarts = pends - padded
    dest = pstarts[e_sorted] + (jnp.arange(n_assign) - starts[e_sorted])
    n_buf = n_assign + N_EXPERTS * MOE_BLOCK
    n_blk = n_buf // MOE_BLOCK
    buf_tok = jnp.zeros((n_buf,), jnp.int32).at[dest].set(tok_flat[order])
    buf_w = jnp.zeros((n_buf,), h.dtype).at[dest].set(w_flat[order])
    blk_e = jnp.minimum(jnp.searchsorted(pends, jnp.arange(n_blk) * MOE_BLOCK, side='right'), N_EXPERTS - 1)

    def expert_block(args):
        tok, wt, e = args
        xb = hf[tok]
        hid = clamped_swiglu(xb @ w_gate[e] + b_gate[e], xb @ w_up[e] + b_up[e])
        return (hid @ w_down[e] + b_down[e]) * wt[:, None]

    out = lax.map(expert_block, (buf_tok.reshape(n_blk, MOE_BLOCK), buf_w.reshape(n_blk, MOE_BLOCK), blk_e))
    y = jax.ops.segment_sum(out.reshape(n_buf, d), buf_tok, num_segments=n_tok)
    return y.reshape(b, s, d)


def setup_inputs(seed: int = 0) -> dict:
    key = jax.random.key(seed)
    ks = jax.random.split(key, 32)
    f32 = jnp.float32
    L, D, E, F = DEPTH, D_MODEL, N_EXPERTS, D_EXPERT
    cin = CMP_BLOCK * HEAD_DIM

    def nrm(k, shape, sc):
        return jax.random.normal(k, shape, f32) * sc

    return {
        'x': nrm(ks[0], (BATCH, SEQ, D), 1.0),
        'c': nrm(ks[1], (BATCH, D), 1.0),
        'norm1_g': 1.0 + nrm(ks[2], (L, D), 0.05),
        'norm2_g': 1.0 + nrm(ks[3], (L, D), 0.05),
        'w_ada': nrm(ks[4], (L, D, 6 * D), 0.5 * D ** -0.5),
        'b_ada': nrm(ks[5], (L, 6 * D), 0.02),
        'w_in': nrm(ks[6], (L, D, IN_WIDTH), D ** -0.5),
        'pool_w': nrm(ks[7], (L, len(POOL_WINDOWS), POOL_GROUP, POOL_GROUP), POOL_GROUP ** -0.5),
        'pool_scale': 1.0 + nrm(ks[8], (L, POOL_WIDTH), 0.1),
        'cmp_pos_k': nrm(ks[9], (L, CMP_BLOCK, HEAD_DIM), 0.1),
        'cmp_w1_k': nrm(ks[10], (L, cin, CMP_HIDDEN), cin ** -0.5),
        'cmp_b1_k': nrm(ks[11], (L, CMP_HIDDEN), 0.02),
        'cmp_w2_k': nrm(ks[12], (L, CMP_HIDDEN, HEAD_DIM), CMP_HIDDEN ** -0.5),
        'cmp_b2_k': nrm(ks[13], (L, HEAD_DIM), 0.02),
        'cmp_pos_v': nrm(ks[14], (L, CMP_BLOCK, HEAD_DIM), 0.1),
        'cmp_w1_v': nrm(ks[15], (L, cin, CMP_HIDDEN), cin ** -0.5),
        'cmp_b1_v': nrm(ks[16], (L, CMP_HIDDEN), 0.02),
        'cmp_w2_v': nrm(ks[17], (L, CMP_HIDDEN, HEAD_DIM), CMP_HIDDEN ** -0.5),
        'cmp_b2_v': nrm(ks[18], (L, HEAD_DIM), 0.02),
        'w_out': nrm(ks[19], (L, D_MIX, D), D_MIX ** -0.5),
        'w_router': nrm(ks[20], (L, D, E), D ** -0.5),
        'b_router': nrm(ks[21], (L, E), 0.01),
        'w_gate': nrm(ks[22], (L, E, D, F), D ** -0.5),
        'b_gate': nrm(ks[23], (L, E, F), 0.01),
        'w_up': nrm(ks[24], (L, E, D, F), D ** -0.5),
        'b_up': nrm(ks[25], (L, E, F), 0.01),
        'w_down': nrm(ks[26], (L, E, F, D), F ** -0.5),
        'b_down': nrm(ks[27], (L, E, D), 0.01),
        'final_g': 1.0 + nrm(ks[28], (D,), 0.05),
    }


def reference(x, c, norm1_g, norm2_g, w_ada, b_ada, w_in, pool_w, pool_scale,
              cmp_pos_k, cmp_w1_k, cmp_b1_k, cmp_w2_k, cmp_b2_k,
              cmp_pos_v, cmp_w1_v, cmp_b1_v, cmp_w2_v, cmp_b2_v,
              w_out, w_router, b_router, w_gate, b_gate, w_up, b_up, w_down, b_down, final_g):
    c_act = jax.nn.silu(c)
    for l in range(DEPTH):
        mod = (c_act @ w_ada[l] + b_ada[l])[:, None, :]
        sh1, sc1, g1, sh2, sc2, g2 = jnp.split(mod, 6, axis=-1)
        h = rmsnorm(x, norm1_g[l]) * (1 + sc1) + sh1
        cmp_k = (cmp_pos_k[l], cmp_w1_k[l], cmp_b1_k[l], cmp_w2_k[l], cmp_b2_k[l])
        cmp_v = (cmp_pos_v[l], cmp_w1_v[l], cmp_b1_v[l], cmp_w2_v[l], cmp_b2_v[l])
        x = x + g1 * hybrid_mixer(h, w_in[l], pool_w[l], pool_scale[l], cmp_k, cmp_v, w_out[l])
        h = rmsnorm(x, norm2_g[l]) * (1 + sc2) + sh2
        x = x + g2 * moe(h, w_router[l], b_router[l], w_gate[l], b_gate[l], w_up[l], b_up[l], w_down[l], b_down[l])
    return rmsnorm(x, final_g)
```

```python
import functools

import jax
import jax.numpy as jnp
from jax import lax
from jax.experimental import pallas as pl
from jax.experimental.pallas import tpu as pltpu

F32 = jnp.float32
BF16 = jnp.bfloat16
I32 = jnp.int32

HEAD_DIM = 128
N_HEADS = 8
N_KV_GROUPS = 2
HEADS_PER_GROUP = N_HEADS // N_KV_GROUPS
POOL_WINDOWS = (2, 4, 8, 16)
POOL_GROUP = 256
POOL_WIDTH = POOL_GROUP * len(POOL_WINDOWS)
NSA_WIDTH = N_HEADS * HEAD_DIM
KV_WIDTH = N_KV_GROUPS * HEAD_DIM
N_GATE_COLS = 3 * N_HEADS
CMP_BLOCK = 32
CMP_STRIDE = 16
SEL_BLOCK = 64
SEL_RATIO = SEL_BLOCK // CMP_STRIDE
N_SELECT = 16
WINDOW = 512
Q_BLOCK = 128
TOP_K = 4
SWIGLU_ALPHA = 1.702
SWIGLU_LIMIT = 7.0
NORM_EPS = 1e-5
NEG = -1e30
BIG = 1e30
BELOW_NEG = -3e38

LANES = 128
SUBLANES = 8
VMEM_LIMIT_BYTES = 56 * 1024 * 1024

MOD_TN = 1024
PROJ_TM = 512
SLC_CHUNK = 256
WIN_CHUNK = 128
DISPATCH_TM = 256
COMBINE_TM = 128
EXPERT_TM = 1024
EXPERT_TF = 256


def _params(*sem):
    return pltpu.CompilerParams(dimension_semantics=sem, vmem_limit_bytes=VMEM_LIMIT_BYTES)


def _dot(a, b):
    return jnp.dot(a, b, preferred_element_type=F32)


def _rmsnorm(x, g):
    return x * lax.rsqrt(jnp.mean(x * x, axis=-1, keepdims=True) + NORM_EPS) * g


def _mod_kernel(c_ref, w_ref, b_ref, o_ref):
    c = c_ref[...]
    ca = (c * jax.nn.sigmoid(c)).astype(BF16)
    o_ref[0] = _dot(ca, w_ref[0].astype(BF16)) + b_ref[0]


def _modulation(c_pad, w_ada, b_ada):
    depth, d, n = w_ada.shape
    rows = c_pad.shape[0]
    return pl.pallas_call(
        _mod_kernel,
        grid=(depth, n // MOD_TN),
        in_specs=[
            pl.BlockSpec((rows, d), lambda l, j: (0, 0)),
            pl.BlockSpec((1, d, MOD_TN), lambda l, j: (l, 0, j)),
            pl.BlockSpec((1, 1, MOD_TN), lambda l, j: (l, 0, j)),
        ],
        out_specs=pl.BlockSpec((1, rows, MOD_TN), lambda l, j: (l, 0, j)),
        out_shape=jax.ShapeDtypeStruct((depth, rows, n), F32),
        compiler_params=_params("arbitrary", "arbitrary"),
    )(c_pad, w_ada, b_ada.reshape(depth, 1, n))


def _inproj_kernel(x_ref, g_ref, sc_ref, sh_ref, w_ref, wg_ref,
                   upool_ref, qT_ref, kvc_ref, ksl_ref, vslT_ref, kwin_ref, vwinT_ref, gT_ref):
    tm = x_ref.shape[1]
    h = _rmsnorm(x_ref[0], g_ref[...]) * (1.0 + sc_ref[0]) + sh_ref[0]
    hb = h.astype(BF16)
    upool_ref[0] = _dot(hb, w_ref[:, 0:POOL_WIDTH])
    q0 = POOL_WIDTH
    scale = HEAD_DIM ** -0.5
    for hd in range(N_HEADS):
        q = _dot(hb, w_ref[:, q0 + hd * HEAD_DIM:q0 + (hd + 1) * HEAD_DIM]) * scale
        qT_ref[0, hd * HEAD_DIM:(hd + 1) * HEAD_DIM, :] = q.T.astype(BF16)
    c0 = q0 + NSA_WIDTH
    for i in range(2 * N_KV_GROUPS):
        kvc_ref[0, i] = _dot(hb, w_ref[:, c0 + i * HEAD_DIM:c0 + (i + 1) * HEAD_DIM])
    s0 = c0 + 2 * KV_WIDTH
    w0 = s0 + 2 * KV_WIDTH
    for g in range(N_KV_GROUPS):
        ksl_ref[0, g] = _dot(hb, w_ref[:, s0 + g * HEAD_DIM:s0 + (g + 1) * HEAD_DIM]).astype(BF16)
        v = _dot(hb, w_ref[:, s0 + KV_WIDTH + g * HEAD_DIM:s0 + KV_WIDTH + (g + 1) * HEAD_DIM])
        vT = v.T.astype(BF16)
        for c in range(tm // SLC_CHUNK):
            vslT_ref[0, g, c] = vT[:, c * SLC_CHUNK:(c + 1) * SLC_CHUNK]
        kwin_ref[0, g] = _dot(hb, w_ref[:, w0 + g * HEAD_DIM:w0 + (g + 1) * HEAD_DIM]).astype(BF16)
        v = _dot(hb, w_ref[:, w0 + KV_WIDTH + g * HEAD_DIM:w0 + KV_WIDTH + (g + 1) * HEAD_DIM])
        vT = v.T.astype(BF16)
        for c in range(tm // WIN_CHUNK):
            vwinT_ref[0, g, c] = vT[:, c * WIN_CHUNK:(c + 1) * WIN_CHUNK]
    gl = jax.nn.sigmoid(_dot(hb, wg_ref[...]))
    gT_ref[0] = gl.T[0:N_GATE_COLS, :]


def _input_projection(x, norm_g, scale, shift, w_main, w_gates):
    b, s, d = x.shape
    tm = PROJ_TM
    n_main = w_main.shape[1]
    tok = lambda bi, m: (bi, m, 0)
    per_b = lambda bi, m: (bi, 0, 0)
    const2 = lambda bi, m: (0, 0)
    out_shape = (
        jax.ShapeDtypeStruct((b, s, POOL_WIDTH), F32),
        jax.ShapeDtypeStruct((b, NSA_WIDTH, s), BF16),
        jax.ShapeDtypeStruct((b, 2 * N_KV_GROUPS, s, HEAD_DIM), F32),
        jax.ShapeDtypeStruct((b, N_KV_GROUPS, s, HEAD_DIM), BF16),
        jax.ShapeDtypeStruct((b, N_KV_GROUPS, s // SLC_CHUNK, HEAD_DIM, SLC_CHUNK), BF16),
        jax.ShapeDtypeStruct((b, N_KV_GROUPS, s, HEAD_DIM), BF16),
        jax.ShapeDtypeStruct((b, N_KV_GROUPS, s // WIN_CHUNK, HEAD_DIM, WIN_CHUNK), BF16),
        jax.ShapeDtypeStruct((b, N_GATE_COLS, s), F32),
    )
    out_specs = (
        pl.BlockSpec((1, tm, POOL_WIDTH), tok),
        pl.BlockSpec((1, NSA_WIDTH, tm), lambda bi, m: (bi, 0, m)),
        pl.BlockSpec((1, 2 * N_KV_GROUPS, tm, HEAD_DIM), lambda bi, m: (bi, 0, m, 0)),
        pl.BlockSpec((1, N_KV_GROUPS, tm, HEAD_DIM), lambda bi, m: (bi, 0, m, 0)),
        pl.BlockSpec((1, N_KV_GROUPS, tm // SLC_CHUNK, HEAD_DIM, SLC_CHUNK), lambda bi, m: (bi, 0, m, 0, 0)),
        pl.BlockSpec((1, N_KV_GROUPS, tm, HEAD_DIM), lambda bi, m: (bi, 0, m, 0)),
        pl.BlockSpec((1, N_KV_GROUPS, tm // WIN_CHUNK, HEAD_DIM, WIN_CHUNK), lambda bi, m: (bi, 0, m, 0, 0)),
        pl.BlockSpec((1, N_GATE_COLS, tm), lambda bi, m: (bi, 0, m)),
    )
    return pl.pallas_call(
        _inproj_kernel,
        grid=(b, s // tm),
        in_specs=[
            pl.BlockSpec((1, tm, d), tok),
            pl.BlockSpec((1, d), const2),
            pl.BlockSpec((1, 1, d), per_b),
            pl.BlockSpec((1, 1, d), per_b),
            pl.BlockSpec((d, n_main), const2, pipeline_mode=pl.Buffered(1)),
            pl.BlockSpec((d, LANES), const2, pipeline_mode=pl.Buffered(1)),
        ],
        out_specs=out_specs,
        out_shape=out_shape,
        compiler_params=_params("arbitrary", "arbitrary"),
    )(x, norm_g, scale, shift, w_main, w_gates)


def _pool_kernel(u_ref, pw_ref, ps_ref, o_ref, ext_ref):
    tm = u_ref.shape[1]
    halo = max(POOL_WINDOWS)
    m = pl.program_id(1)

    @pl.when(m == 0)
    def _():
        ext_ref[0:halo, :] = jnp.zeros((halo, POOL_WIDTH), F32)

    ext_ref[halo:halo + tm, :] = u_ref[0]
    t = m * tm + lax.broadcasted_iota(I32, (tm, 1), 0)
    for gi, w in enumerate(POOL_WINDOWS):
        cols = slice(gi * POOL_GROUP, (gi + 1) * POOL_GROUP)
        ug = ext_ref[halo:halo + tm, cols]
        acc = ug
        for i in range(1, w):
            acc = acc + ext_ref[halo - i:halo - i + tm, cols]
        cnt = jnp.minimum(t + 1, w).astype(F32)
        diff = (acc / cnt - ug).astype(BF16)
        o_ref[0, :, cols] = (_dot(diff, pw_ref[gi]) * ps_ref[:, cols]).astype(BF16)
    ext_ref[0:halo, :] = ext_ref[tm:tm + halo, :]


def _pool_mixer(u_pool, pool_w, pool_scale):
    b, s, _ = u_pool.shape
    tm = PROJ_TM
    return pl.pallas_call(
        _pool_kernel,
        grid=(b, s // tm),
        in_specs=[
            pl.BlockSpec((1, tm, POOL_WIDTH), lambda bi, m: (bi, m, 0)),
            pl.BlockSpec((len(POOL_WINDOWS), POOL_GROUP, POOL_GROUP), lambda bi, m: (0, 0, 0)),
            pl.BlockSpec((1, POOL_WIDTH), lambda bi, m: (0, 0)),
        ],
        out_specs=pl.BlockSpec((1, tm, POOL_WIDTH), lambda bi, m: (bi, m, 0)),
        out_shape=jax.ShapeDtypeStruct((b, s, POOL_WIDTH), BF16),
        scratch_shapes=[pltpu.VMEM((tm + max(POOL_WINDOWS), POOL_WIDTH), F32)],
        compiler_params=_params("arbitrary", "arbitrary"),
    )(u_pool, pool_w, pool_scale)


def _compress_kernel(a_ref, pos_ref, w1_ref, b1_ref, w2_ref, b2_ref, o_ref, oT_ref):
    half = CMP_STRIDE * HEAD_DIM
    a = a_ref[0, 0]
    nb = a.shape[0]
    pos = pos_ref[0]
    a_lo = (a + pos[:, 0:half]).astype(BF16)
    a_hi = (a + pos[:, half:2 * half]).astype(BF16)
    p_lo = _dot(a_lo, w1_ref[0, 0:half, :])
    p_hi = _dot(a_hi, w1_ref[0, half:2 * half, :])
    hid = jax.nn.gelu(p_lo + pltpu.roll(p_hi, nb - 1, 0) + b1_ref[0])
    out = _dot(hid.astype(BF16), w2_ref[0]) + b2_ref[0]
    o_ref[0, 0] = out.astype(BF16)
    oT_ref[0, 0] = out.T.astype(BF16)


def _compress(kvc, pos, w1, b1, w2, b2):
    b, n4, s, _ = kvc.shape
    nb = s // CMP_STRIDE
    a = kvc.reshape(b, n4, nb, CMP_STRIDE * HEAD_DIM)
    hidden = w1.shape[-1]
    kv = lambda bi, i: (i // N_KV_GROUPS, 0, 0)
    return pl.pallas_call(
        _compress_kernel,
        grid=(b, n4),
        in_specs=[
            pl.BlockSpec((1, 1, nb, CMP_STRIDE * HEAD_DIM), lambda bi, i: (bi, i, 0, 0)),
            pl.BlockSpec((1, 1, CMP_BLOCK * HEAD_DIM), kv),
            pl.BlockSpec((1, CMP_BLOCK * HEAD_DIM, hidden), kv),
            pl.BlockSpec((1, 1, hidden), kv),
            pl.BlockSpec((1, hidden, HEAD_DIM), kv),
            pl.BlockSpec((1, 1, HEAD_DIM), kv),
        ],
        out_specs=(
            pl.BlockSpec((1, 1, nb, HEAD_DIM), lambda bi, i: (bi, i, 0, 0)),
            pl.BlockSpec((1, 1, HEAD_DIM, nb), lambda bi, i: (bi, i, 0, 0)),
        ),
        out_shape=(
            jax.ShapeDtypeStruct((b, n4, nb, HEAD_DIM), BF16),
            jax.ShapeDtypeStruct((b, n4, HEAD_DIM, nb), BF16),
        ),
        compiler_params=_params("arbitrary", "arbitrary"),
    )(a, pos, w1, b1, w2, b2)


def _online_update(s, v_t, m_old, l_old, acc_ref):
    m_new = jnp.maximum(m_old, jnp.max(s, axis=0, keepdims=True))
    alpha = jnp.exp(m_old - m_new)
    p = jnp.exp(s - m_new)
    l_new = alpha * l_old + jnp.sum(p, axis=0, keepdims=True)
    acc_ref[...] = acc_ref[...] * alpha + _dot(v_t, p.astype(BF16))
    return m_new, l_new


def _nsa_kernel(qT_ref, kc_ref, vcT_ref, ksl_ref, vslT_ref, kwin_ref, vwinT_ref, gT_ref, y_ref,
                qcat_ref, pbuf_ref, score_ref, sel_ref, sel4_ref, ocmp_ref, aslc_ref, awin_ref):
    n = pl.program_id(2)
    q_start = n * Q_BLOCK
    nq = HEADS_PER_GROUP * Q_BLOCK
    nb = kc_ref.shape[2]
    ns = sel_ref.shape[0]

    for r in range(HEADS_PER_GROUP):
        qcat_ref[:, r * Q_BLOCK:(r + 1) * Q_BLOCK] = qT_ref[0, r * HEAD_DIM:(r + 1) * HEAD_DIM, :]
    qc = qcat_ref[...]
    tok = q_start + (lax.broadcasted_iota(I32, (1, nq), 1) & (Q_BLOCK - 1))

    s = _dot(kc_ref[0, 0], qc)
    ci = lax.broadcasted_iota(I32, (nb, nq), 0)
    s = jnp.where(ci * CMP_STRIDE + (CMP_BLOCK - 1) <= tok, s, NEG)
    m = jnp.max(s, axis=0, keepdims=True)
    e = jnp.exp(s - m)
    l = jnp.sum(e, axis=0, keepdims=True)
    inv = jnp.where(tok >= CMP_BLOCK - 1, 1.0 / l, 0.0)
    p = e * inv
    ocmp_ref[...] = _dot(vcT_ref[0, 0], p.astype(BF16))

    psum = p[:, 0:Q_BLOCK]
    for r in range(1, HEADS_PER_GROUP):
        psum = psum + p[:, r * Q_BLOCK:(r + 1) * Q_BLOCK]
    pbuf_ref[0:SUBLANES, :] = jnp.zeros((SUBLANES, Q_BLOCK), F32)
    pbuf_ref[SUBLANES:SUBLANES + nb, :] = psum
    imp = pbuf_ref[pl.ds(SUBLANES - 1, ns, stride=SEL_RATIO), :]
    for d in range(SEL_RATIO):
        imp = imp + pbuf_ref[pl.ds(SUBLANES + d, ns, stride=SEL_RATIO), :]

    ji = lax.broadcasted_iota(I32, (ns, Q_BLOCK), 0)
    t1 = q_start + lax.broadcasted_iota(I32, (1, Q_BLOCK), 1)
    tj = t1 // SEL_BLOCK
    forced = (ji == 0) | (ji == tj) | (ji == tj - 1)
    score_ref[...] = jnp.where(forced, BIG, jnp.where(ji * SEL_BLOCK <= t1, imp, NEG))
    sel_ref[...] = jnp.zeros((ns, Q_BLOCK), F32)

    def pick(_, carry):
        sc = score_ref[...]
        mx = jnp.max(sc, axis=0, keepdims=True)
        first = jnp.min(jnp.where(sc == mx, ji, ns), axis=0, keepdims=True)
        hit = ji == first
        sel_ref[...] = jnp.where(hit & (mx >= 0.0), 1.0, sel_ref[...])
        score_ref[...] = jnp.where(hit, BELOW_NEG, sc)
        return carry

    lax.fori_loop(0, min(N_SELECT, ns), pick, 0)
    sel = sel_ref[...]
    for r in range(HEADS_PER_GROUP):
        sel4_ref[:, r * Q_BLOCK:(r + 1) * Q_BLOCK] = sel

    m0 = jnp.full((1, nq), NEG, F32)
    l0 = jnp.zeros((1, nq), F32)

    aslc_ref[...] = jnp.zeros(aslc_ref.shape, F32)
    blocks_per_chunk = SLC_CHUNK // SEL_BLOCK

    def slc_step(c, carry):
        k0 = pl.multiple_of(c * SLC_CHUNK, SLC_CHUNK)
        sc = _dot(ksl_ref[0, 0, pl.ds(k0, SLC_CHUNK), :], qc)
        kpos = k0 + lax.broadcasted_iota(I32, (SEL_BLOCK, nq), 0)
        pieces = []
        for j in range(blocks_per_chunk):
            picked = sel4_ref[pl.ds(c * blocks_per_chunk + j, 1), :] > 0.5
            ok = picked & (kpos + j * SEL_BLOCK <= tok)
            pieces.append(jnp.where(ok, sc[j * SEL_BLOCK:(j + 1) * SEL_BLOCK], NEG))
        sc = jnp.concatenate(pieces, axis=0)
        return _online_update(sc, vslT_ref[0, 0, c], carry[0], carry[1], aslc_ref)

    n_slc = (q_start + Q_BLOCK + SLC_CHUNK - 1) // SLC_CHUNK
    _, l_slc = lax.fori_loop(0, n_slc, slc_step, (m0, l0))

    awin_ref[...] = jnp.zeros(awin_ref.shape, F32)

    def win_step(i, carry):
        wb = n - i
        k0 = pl.multiple_of(wb * WIN_CHUNK, WIN_CHUNK)
        sc = _dot(kwin_ref[0, 0, pl.ds(k0, WIN_CHUNK), :], qc)
        dist = tok - (k0 + lax.broadcasted_iota(I32, (WIN_CHUNK, nq), 0))
        sc = jnp.where((dist >= 0) & (dist < WINDOW), sc, NEG)
        return _online_update(sc, vwinT_ref[0, 0, wb], carry[0], carry[1], awin_ref)

    n_win = jnp.minimum(n, WINDOW // WIN_CHUNK) + 1
    _, l_win = lax.fori_loop(0, n_win, win_step, (m0, l0))

    gates = gT_ref[0, 0]
    inv_slc = 1.0 / l_slc
    inv_win = 1.0 / l_win
    for r in range(HEADS_PER_GROUP):
        cols = slice(r * Q_BLOCK, (r + 1) * Q_BLOCK)
        o = (gates[3 * r:3 * r + 1] * ocmp_ref[:, cols]
             + gates[3 * r + 1:3 * r + 2] * inv_slc[:, cols] * aslc_ref[:, cols]
             + gates[3 * r + 2:3 * r + 3] * inv_win[:, cols] * awin_ref[:, cols])
        y_ref[0, :, r * HEAD_DIM:(r + 1) * HEAD_DIM] = o.T.astype(BF16)


def _nsa(qT, kc, kcT, ksl, vslT, kwin, vwinT, gT):
    b, _, s = qT.shape
    nb = kc.shape[2]
    ns = s // SEL_BLOCK
    g4 = HEADS_PER_GROUP * HEAD_DIM
    nq = HEADS_PER_GROUP * Q_BLOCK
    gates = gT.reshape(b, N_KV_GROUPS, HEADS_PER_GROUP * 3, s)
    per_bg = lambda bi, g, n: (bi, g, 0, 0)
    per_bg5 = lambda bi, g, n: (bi, g, 0, 0, 0)
    return pl.pallas_call(
        _nsa_kernel,
        grid=(b, N_KV_GROUPS, s // Q_BLOCK),
        in_specs=[
            pl.BlockSpec((1, g4, Q_BLOCK), lambda bi, g, n: (bi, g, n)),
            pl.BlockSpec((1, 1, nb, HEAD_DIM), per_bg),
            pl.BlockSpec((1, 1, HEAD_DIM, nb), lambda bi, g, n: (bi, N_KV_GROUPS + g, 0, 0)),
            pl.BlockSpec((1, 1, s, HEAD_DIM), per_bg),
            pl.BlockSpec((1, 1, s // SLC_CHUNK, HEAD_DIM, SLC_CHUNK), per_bg5),
            pl.BlockSpec((1, 1, s, HEAD_DIM), per_bg),
            pl.BlockSpec((1, 1, s // WIN_CHUNK, HEAD_DIM, WIN_CHUNK), per_bg5),
            pl.BlockSpec((1, 1, HEADS_PER_GROUP * 3, Q_BLOCK), lambda bi, g, n: (bi, g, 0, n)),
        ],
        out_specs=pl.BlockSpec((1, Q_BLOCK, g4), lambda bi, g, n: (bi, n, g)),
        out_shape=jax.ShapeDtypeStruct((b, s, NSA_WIDTH), BF16),
        scratch_shapes=[
            pltpu.VMEM((HEAD_DIM, nq), BF16),
            pltpu.VMEM((nb + SUBLANES, Q_BLOCK), F32),
            pltpu.VMEM((ns, Q_BLOCK), F32),
            pltpu.VMEM((ns, Q_BLOCK), F32),
            pltpu.VMEM((ns, nq), F32),
            pltpu.VMEM((HEAD_DIM, nq), F32),
            pltpu.VMEM((HEAD_DIM, nq), F32),
            pltpu.VMEM((HEAD_DIM, nq), F32),
        ],
        compiler_params=_params("arbitrary", "arbitrary", "arbitrary"),
    )(qT, kc, kcT, ksl, vslT, kwin, vwinT, gates)


def _outproj_kernel(yp_ref, yn_ref, wo_ref, x_ref, g1_ref, n2_ref, sc_ref, sh_ref, wrT_ref, br_ref, tri_ref,
                    x1_ref, h2_ref, idx_ref, wt_ref, rank_ref, cnt_ref, carry_ref):
    tm = x_ref.shape[1]
    n_exp = wrT_ref.shape[0]

    @pl.when((pl.program_id(0) == 0) & (pl.program_id(1) == 0))
    def _():
        carry_ref[...] = jnp.zeros(carry_ref.shape, F32)

    mix = _dot(yp_ref[0], wo_ref[0:POOL_WIDTH, :]) + _dot(yn_ref[0], wo_ref[POOL_WIDTH:POOL_WIDTH + NSA_WIDTH, :])
    x1 = x_ref[0] + g1_ref[0] * mix
    x1_ref[0] = x1
    h2 = _rmsnorm(x1, n2_ref[...]) * (1.0 + sc_ref[0]) + sh_ref[0]
    h2_ref[0] = h2

    logits = lax.dot_general(wrT_ref[...], h2.astype(BF16), (((1,), (1,)), ((), ())),
                             preferred_element_type=F32) + br_ref[...]
    ei = lax.broadcasted_iota(I32, (n_exp, tm), 0)
    vals, ids = [], []
    for _ in range(TOP_K):
        mx = jnp.max(logits, axis=0, keepdims=True)
        first = jnp.min(jnp.where(logits == mx, ei, n_exp), axis=0, keepdims=True)
        vals.append(mx)
        ids.append(first)
        logits = jnp.where(ei == first, -jnp.inf, logits)
    exps = [jnp.exp(v - vals[0]) for v in vals]
    den = exps[0]
    for e in exps[1:]:
        den = den + e

    onehot = jnp.zeros((n_exp, tm), F32)
    for first in ids:
        onehot = onehot + (ei == first).astype(F32)
    before = _dot(onehot.astype(BF16), tri_ref[...]) + carry_ref[:, 0:1]
    wt_ref[TOP_K:SUBLANES, :] = jnp.zeros((SUBLANES - TOP_K, tm), F32)
    for k in range(TOP_K):
        idx_ref[k:k + 1, :] = ids[k]
        wt_ref[k:k + 1, :] = exps[k] / den
        rank = jnp.sum(jnp.where(ei == ids[k], before, 0.0), axis=0, keepdims=True)
        rank_ref[k:k + 1, :] = rank.astype(I32)
    carry_ref[...] = carry_ref[...] + jnp.sum(onehot, axis=1, keepdims=True)
    cnt_ref[...] = carry_ref[...].astype(I32)


def _output_projection(y_pool, y_nsa, w_out, x, gate1, norm_g, scale, shift, w_router_t, b_router, tri):
    b, s, d = x.shape
    tm = PROJ_TM
    n_exp = w_router_t.shape[0]
    t_all = b * s
    tok = lambda bi, m: (bi, m, 0)
    per_b = lambda bi, m: (bi, 0, 0)
    const2 = lambda bi, m: (0, 0)
    flat = lambda bi, m: (0, bi * (s // tm) + m)
    return pl.pallas_call(
        _outproj_kernel,
        grid=(b, s // tm),
        in_specs=[
            pl.BlockSpec((1, tm, POOL_WIDTH), tok),
            pl.BlockSpec((1, tm, NSA_WIDTH), tok),
            pl.BlockSpec((POOL_WIDTH + NSA_WIDTH, d), const2, pipeline_mode=pl.Buffered(1)),
            pl.BlockSpec((1, tm, d), tok),
            pl.BlockSpec((1, 1, d), per_b),
            pl.BlockSpec((1, d), const2),
            pl.BlockSpec((1, 1, d), per_b),
            pl.BlockSpec((1, 1, d), per_b),
            pl.BlockSpec((n_exp, d), const2),
            pl.BlockSpec((n_exp, 1), const2),
            pl.BlockSpec((tm, tm), const2, pipeline_mode=pl.Buffered(1)),
        ],
        out_specs=(
            pl.BlockSpec((1, tm, d), tok),
            pl.BlockSpec((1, tm, d), tok),
            pl.BlockSpec((TOP_K, tm), flat),
            pl.BlockSpec((SUBLANES, tm), flat),
            pl.BlockSpec((TOP_K, tm), flat),
            pl.BlockSpec((n_exp, LANES), const2),
        ),
        out_shape=(
            jax.ShapeDtypeStruct((b, s, d), F32),
            jax.ShapeDtypeStruct((b, s, d), F32),
            jax.ShapeDtypeStruct((TOP_K, t_all), I32),
            jax.ShapeDtypeStruct((SUBLANES, t_all), F32),
            jax.ShapeDtypeStruct((TOP_K, t_all), I32),
            jax.ShapeDtypeStruct((n_exp, LANES), I32),
        ),
        scratch_shapes=[pltpu.VMEM((n_exp, LANES), F32)],
        compiler_params=_params("arbitrary", "arbitrary"),
    )(y_pool, y_nsa, w_out, x, gate1, norm_g, scale, shift, w_router_t, b_router, tri)


def _dest_kernel(ps_ref, idx_ref, rank_ref, dest_ref):
    idx = idx_ref[...]
    dest = rank_ref[...]
    for e in range(ps_ref.shape[0]):
        dest = dest + jnp.where(idx == e, ps_ref[e], 0)
    dest_ref[...] = dest


def _destinations(seg_starts, idx_t, rank_t):
    return pl.pallas_call(
        _dest_kernel,
        in_specs=[
            pl.BlockSpec(memory_space=pltpu.SMEM),
            pl.BlockSpec(memory_space=pltpu.VMEM),
            pl.BlockSpec(memory_space=pltpu.VMEM),
        ],
        out_specs=pl.BlockSpec(memory_space=pltpu.VMEM),
        out_shape=jax.ShapeDtypeStruct(idx_t.shape, I32),
    )(seg_starts, idx_t, rank_t)


def _row_copy(src_ref, src_row, dst_ref, dst_row, sem):
    return pltpu.make_async_copy(src_ref.at[pl.ds(src_row, 1)], dst_ref.at[pl.ds(dst_row, 1)], sem)


def _dispatch_kernel(dest_ref, h_ref, xs_in_ref, xs_ref, sem):
    del xs_in_ref
    tm = h_ref.shape[0]

    def issue(t, carry):
        for k in range(TOP_K):
            _row_copy(h_ref, t, xs_ref, dest_ref[k, t], sem).start()
        return carry

    lax.fori_loop(0, tm, issue, 0)

    def drain(t, carry):
        for k in range(TOP_K):
            _row_copy(h_ref, t, xs_ref, dest_ref[k, t], sem).wait()
        return carry

    lax.fori_loop(0, tm, drain, 0)


def _dispatch(dest_t, h_rows, xs_zero):
    t_all, d = h_rows.shape
    tm = DISPATCH_TM
    return pl.pallas_call(
        _dispatch_kernel,
        grid=(t_all // tm,),
        in_specs=[
            pl.BlockSpec((TOP_K, tm), lambda i: (0, i), memory_space=pltpu.SMEM),
            pl.BlockSpec((tm, d), lambda i: (i, 0)),
            pl.BlockSpec(memory_space=pl.ANY),
        ],
        out_specs=pl.BlockSpec(memory_space=pl.ANY),
        out_shape=jax.ShapeDtypeStruct(xs_zero.shape, xs_zero.dtype),
        scratch_shapes=[pltpu.SemaphoreType.DMA(())],
        input_output_aliases={2: 0},
        compiler_params=_params("arbitrary"),
    )(dest_t, h_rows, xs_zero)


def _ffn_kernel(be_ref, nu_ref, xs_ref, wg_ref, bg_ref, wu_ref, bu_ref, wd_ref, bd_ref, y_ref, xb_ref):
    del be_ref
    i = pl.program_id(0)
    j = pl.program_id(1)
    used = i < nu_ref[0]

    @pl.when(used & (j == 0))
    def _():
        xb_ref[...] = xs_ref[...].astype(BF16)

    @pl.when(used)
    def _():
        xb = xb_ref[...]
        gl = jnp.minimum(_dot(xb, wg_ref[0].astype(BF16)) + bg_ref[0], SWIGLU_LIMIT)
        lin = jnp.clip(_dot(xb, wu_ref[0].astype(BF16)) + bu_ref[0], -SWIGLU_LIMIT, SWIGLU_LIMIT)
        hid = gl * jax.nn.sigmoid(SWIGLU_ALPHA * gl) * (lin + 1.0)
        part = _dot(hid.astype(BF16), wd_ref[0].astype(BF16))

        @pl.when(j == 0)
        def _():
            y_ref[...] = part + bd_ref[0]

        @pl.when(j > 0)
        def _():
            y_ref[...] = y_ref[...] + part

    @pl.when(jnp.logical_not(used) & (j == 0))
    def _():
        y_ref[...] = jnp.zeros(y_ref.shape, F32)


def _expert_ffn(block_expert, n_used, xs, w_gate, b_gate, w_up, b_up, w_down, b_down):
    n_buf, d = xs.shape
    n_exp, _, f = w_gate.shape
    tm, tf = EXPERT_TM, EXPERT_TF
    nf = f // tf
    n_blk = n_buf // tm

    def jj(i, j, nu):
        return jnp.where(i < nu[0], j, nf - 1)

    def rows(i, j, be, nu):
        return (jnp.minimum(i, nu[0] - 1), 0)

    grid_spec = pltpu.PrefetchScalarGridSpec(
        num_scalar_prefetch=2,
        grid=(n_blk, nf),
        in_specs=[
            pl.BlockSpec((tm, d), rows),
            pl.BlockSpec((1, d, tf), lambda i, j, be, nu: (be[i], 0, jj(i, j, nu))),
            pl.BlockSpec((1, 1, tf), lambda i, j, be, nu: (be[i], 0, jj(i, j, nu))),
            pl.BlockSpec((1, d, tf), lambda i, j, be, nu: (be[i], 0, jj(i, j, nu))),
            pl.BlockSpec((1, 1, tf), lambda i, j, be, nu: (be[i], 0, jj(i, j, nu))),
            pl.BlockSpec((1, tf, d), lambda i, j, be, nu: (be[i], jj(i, j, nu), 0)),
            pl.BlockSpec((1, 1, d), lambda i, j, be, nu: (be[i], 0, 0)),
        ],
        out_specs=pl.BlockSpec((tm, d), lambda i, j, be, nu: (i, 0)),
        scratch_shapes=[pltpu.VMEM((tm, d), BF16)],
    )
    return pl.pallas_call(
        _ffn_kernel,
        grid_spec=grid_spec,
        out_shape=jax.ShapeDtypeStruct((n_buf, d), F32),
        compiler_params=_params("arbitrary", "arbitrary"),
    )(block_expert, n_used, xs, w_gate, b_gate.reshape(n_exp, 1, f), w_up, b_up.reshape(n_exp, 1, f),
      w_down, b_down.reshape(n_exp, 1, d))


def _combine_kernel(dest_ref, wt_ref, x1_ref, g2_ref, fg_ref, y_ref, o_ref, rows_ref, sem, *, final_norm):
    tm = x1_ref.shape[1]

    def issue(t, carry):
        for k in range(TOP_K):
            _row_copy(y_ref, dest_ref[k, t], rows_ref.at[k], t, sem).start()
        return carry

    lax.fori_loop(0, tm, issue, 0)

    wt = jnp.concatenate([wt_ref[...], jnp.zeros((tm - SUBLANES, tm), F32)], axis=0).T

    def drain(t, carry):
        for k in range(TOP_K):
            _row_copy(y_ref, dest_ref[k, t], rows_ref.at[k], t, sem).wait()
        return carry

    lax.fori_loop(0, tm, drain, 0)

    acc = wt[:, 0:1] * rows_ref[0]
    for k in range(1, TOP_K):
        acc = acc + wt[:, k:k + 1] * rows_ref[k]
    out = x1_ref[0] + g2_ref[0] * acc
    if final_norm:
        out = _rmsnorm(out, fg_ref[...])
    o_ref[0] = out


def _combine(dest_t, wt_t, x1, gate2, final_g, y_rows, final_norm):
    b, s, d = x1.shape
    tm = COMBINE_TM
    flat = lambda bi, m: (0, bi * (s // tm) + m)
    return pl.pallas_call(
        functools.partial(_combine_kernel, final_norm=final_norm),
        grid=(b, s // tm),
        in_specs=[
            pl.BlockSpec((TOP_K, tm), flat, memory_space=pltpu.SMEM),
            pl.BlockSpec((SUBLANES, tm), flat),
            pl.BlockSpec((1, tm, d), lambda bi, m: (bi, m, 0)),
            pl.BlockSpec((1, 1, d), lambda bi, m: (bi, 0, 0)),
            pl.BlockSpec((1, d), lambda bi, m: (0, 0)),
            pl.BlockSpec(memory_space=pl.ANY),
        ],
        out_specs=pl.BlockSpec((1, tm, d), lambda bi, m: (bi, m, 0)),
        out_shape=jax.ShapeDtypeStruct((b, s, d), F32),
        scratch_shapes=[pltpu.VMEM((TOP_K, tm, d), F32), pltpu.SemaphoreType.DMA(())],
        compiler_params=_params("arbitrary", "arbitrary"),
    )(dest_t, wt_t, x1, gate2, final_g, y_rows)


def _segment_layout(counts, n_blk):
    padded = (counts + EXPERT_TM - 1) // EXPERT_TM * EXPERT_TM
    ends = jnp.cumsum(padded)
    starts = ends - padded
    n_used = ends[-1] // EXPERT_TM
    blk = jnp.minimum(jnp.arange(n_blk, dtype=I32), n_used - 1) * EXPERT_TM
    block_expert = jnp.minimum(jnp.searchsorted(ends, blk, side="right"), counts.shape[0] - 1)
    return starts.astype(I32), block_expert.astype(I32), n_used.reshape(1).astype(I32)


def kernel(x, c, norm1_g, norm2_g, w_ada, b_ada, w_in, pool_w, pool_scale, cmp_pos_k, cmp_w1_k, cmp_b1_k, cmp_w2_k, cmp_b2_k, cmp_pos_v, cmp_w1_v, cmp_b1_v, cmp_w2_v, cmp_b2_v, w_out, w_router, b_router, w_gate, b_gate, w_up, b_up, w_down, b_down, final_g):
    b, s, d = x.shape
    depth = w_ada.shape[0]
    n_exp = w_router.shape[-1]
    t_all = b * s
    assert s % PROJ_TM == 0 and s % SLC_CHUNK == 0 and t_all % DISPATCH_TM == 0
    n_main = POOL_WIDTH + NSA_WIDTH + 6 * KV_WIDTH
    assert w_in.shape[-1] == n_main + N_GATE_COLS

    c_pad = jnp.pad(c, ((0, SUBLANES - b), (0, 0)))
    mod = _modulation(c_pad, w_ada, b_ada)[:, :b]
    tri = (jnp.arange(PROJ_TM)[:, None] < jnp.arange(PROJ_TM)[None, :]).astype(BF16)
    n_buf = (t_all * TOP_K + n_exp * (EXPERT_TM - 1)) // EXPERT_TM * EXPERT_TM
    n_blk = n_buf // EXPERT_TM

    for l in range(depth):
        sh1, sc1, g1, sh2, sc2, g2 = [m.reshape(b, 1, d) for m in jnp.split(mod[l], 6, axis=-1)]
        w_main = w_in[l, :, :n_main].astype(BF16)
        w_gates = jnp.pad(w_in[l, :, n_main:], ((0, 0), (0, LANES - N_GATE_COLS))).astype(BF16)
        u_pool, qT, kvc, ksl, vslT, kwin, vwinT, gT = _input_projection(
            x, norm1_g[l].reshape(1, d), sc1, sh1, w_main, w_gates)
        y_pool = _pool_mixer(u_pool, pool_w[l].astype(BF16), pool_scale[l].reshape(1, POOL_WIDTH))
        kc, kcT = _compress(
            kvc,
            jnp.stack([cmp_pos_k[l], cmp_pos_v[l]]).reshape(2, 1, CMP_BLOCK * HEAD_DIM),
            jnp.stack([cmp_w1_k[l], cmp_w1_v[l]]).astype(BF16),
            jnp.stack([cmp_b1_k[l], cmp_b1_v[l]])[:, None, :],
            jnp.stack([cmp_w2_k[l], cmp_w2_v[l]]).astype(BF16),
            jnp.stack([cmp_b2_k[l], cmp_b2_v[l]])[:, None, :])
        y_nsa = _nsa(qT, kc, kcT, ksl, vslT, kwin, vwinT, gT)
        x1, h2, idx_t, wt_t, rank_t, counts = _output_projection(
            y_pool, y_nsa, w_out[l].astype(BF16), x, g1, norm2_g[l].reshape(1, d), sc2, sh2,
            w_router[l].T.astype(BF16), b_router[l].reshape(n_exp, 1), tri)
        seg_starts, block_expert, n_used = _segment_layout(counts[:, 0], n_blk)
        dest_t = _destinations(seg_starts, idx_t, rank_t)
        xs = _dispatch(dest_t, h2.reshape(t_all, d), jnp.zeros((n_buf, d), F32))
        y_rows = _expert_ffn(block_expert, n_used, xs, w_gate[l], b_gate[l], w_up[l], b_up[l], w_down[l], b_down[l])
        x = _combine(dest_t, wt_t, x1, g2, final_g.reshape(1, d), y_rows, final_norm=(l == depth - 1))
    return x
```

```python
import functools

import jax
import jax.numpy as jnp
from jax import lax
from jax.experimental import pallas as pl
from jax.experimental.pallas import tpu as pltpu

F32 = jnp.float32
BF16 = jnp.bfloat16
I32 = jnp.int32

HEAD_DIM = 128
N_HEADS = 8
N_KV_GROUPS = 2
HEADS_PER_GROUP = N_HEADS // N_KV_GROUPS
POOL_WINDOWS = (2, 4, 8, 16)
POOL_GROUP = 256
POOL_WIDTH = POOL_GROUP * len(POOL_WINDOWS)
NSA_WIDTH = N_HEADS * HEAD_DIM
KV_WIDTH = N_KV_GROUPS * HEAD_DIM
N_GATE_COLS = 3 * N_HEADS
CMP_BLOCK = 32
CMP_STRIDE = 16
SEL_BLOCK = 64
SEL_RATIO = SEL_BLOCK // CMP_STRIDE
N_SELECT = 16
WINDOW = 512
Q_BLOCK = 128
TOP_K = 4
SWIGLU_ALPHA = 1.702
SWIGLU_LIMIT = 7.0
NORM_EPS = 1e-5
NEG = -1e30
BIG = 1e30
BELOW_NEG = -3e38
LOG2_E = 1.4426950408889634

LANES = 128
SUBLANES = 8
VMEM_LIMIT_BYTES = 56 * 1024 * 1024

MOD_TN = 1024
PROJ_TM = 512
SLC_CHUNK = 256
WIN_CHUNK = 128
DISPATCH_TM = 256
DISPATCH_ZERO_ROWS = 256
COMBINE_TM = 128
EXPERT_TM = 1024
EXPERT_TF = 512


def _params(*sem):
    return pltpu.CompilerParams(dimension_semantics=sem, vmem_limit_bytes=VMEM_LIMIT_BYTES)


def _dot(a, b):
    return jnp.dot(a, b, preferred_element_type=F32)


def _rmsnorm(x, g):
    return x * lax.rsqrt(jnp.mean(x * x, axis=-1, keepdims=True) + NORM_EPS) * g


def _mod_kernel(c_ref, w_ref, b_ref, o_ref):
    c = c_ref[...]
    ca = (c * jax.nn.sigmoid(c)).astype(BF16)
    o_ref[0] = _dot(ca, w_ref[0].astype(BF16)) + b_ref[0]


def _modulation(c_pad, w_ada, b_ada):
    depth, d, n = w_ada.shape
    rows = c_pad.shape[0]
    return pl.pallas_call(
        _mod_kernel,
        grid=(depth, n // MOD_TN),
        in_specs=[
            pl.BlockSpec((rows, d), lambda l, j: (0, 0)),
            pl.BlockSpec((1, d, MOD_TN), lambda l, j: (l, 0, j)),
            pl.BlockSpec((1, 1, MOD_TN), lambda l, j: (l, 0, j)),
        ],
        out_specs=pl.BlockSpec((1, rows, MOD_TN), lambda l, j: (l, 0, j)),
        out_shape=jax.ShapeDtypeStruct((depth, rows, n), F32),
        compiler_params=_params("arbitrary", "arbitrary"),
        name="adaln_modulation",
    )(c_pad, w_ada, b_ada.reshape(depth, 1, n))


def _inproj_kernel(x_ref, g_ref, sc_ref, sh_ref, w_ref, wg_ref,
                   upool_ref, qT_ref, kvc_ref, ksl_ref, vslT_ref, kwin_ref, vwinT_ref, gT_ref):
    tm = x_ref.shape[1]
    h = _rmsnorm(x_ref[0], g_ref[...]) * (1.0 + sc_ref[0]) + sh_ref[0]
    hb = h.astype(BF16)
    upool_ref[0] = _dot(hb, w_ref[:, 0:POOL_WIDTH])
    q0 = POOL_WIDTH
    scale = HEAD_DIM ** -0.5 * LOG2_E
    for hd in range(N_HEADS):
        q = _dot(hb, w_ref[:, q0 + hd * HEAD_DIM:q0 + (hd + 1) * HEAD_DIM]) * scale
        qT_ref[0, hd * HEAD_DIM:(hd + 1) * HEAD_DIM, :] = q.T.astype(BF16)
    c0 = q0 + NSA_WIDTH
    for i in range(2 * N_KV_GROUPS):
        kvc_ref[0, i] = _dot(hb, w_ref[:, c0 + i * HEAD_DIM:c0 + (i + 1) * HEAD_DIM])
    s0 = c0 + 2 * KV_WIDTH
    w0 = s0 + 2 * KV_WIDTH
    for g in range(N_KV_GROUPS):
        ksl_ref[0, g] = _dot(hb, w_ref[:, s0 + g * HEAD_DIM:s0 + (g + 1) * HEAD_DIM]).astype(BF16)
        v = _dot(hb, w_ref[:, s0 + KV_WIDTH + g * HEAD_DIM:s0 + KV_WIDTH + (g + 1) * HEAD_DIM])
        vT = v.T.astype(BF16)
        for c in range(tm // SLC_CHUNK):
            vslT_ref[0, g, c] = vT[:, c * SLC_CHUNK:(c + 1) * SLC_CHUNK]
        kwin_ref[0, g] = _dot(hb, w_ref[:, w0 + g * HEAD_DIM:w0 + (g + 1) * HEAD_DIM]).astype(BF16)
        v = _dot(hb, w_ref[:, w0 + KV_WIDTH + g * HEAD_DIM:w0 + KV_WIDTH + (g + 1) * HEAD_DIM])
        vT = v.T.astype(BF16)
        for c in range(tm // WIN_CHUNK):
            vwinT_ref[0, g, c] = vT[:, c * WIN_CHUNK:(c + 1) * WIN_CHUNK]
    gl = jax.nn.sigmoid(_dot(hb, wg_ref[...]))
    gT_ref[0] = gl.T[0:N_GATE_COLS, :]


def _input_projection(x, norm_g, scale, shift, w_main, w_gates):
    b, s, d = x.shape
    tm = PROJ_TM
    n_main = w_main.shape[1]
    tok = lambda bi, m: (bi, m, 0)
    per_b = lambda bi, m: (bi, 0, 0)
    const2 = lambda bi, m: (0, 0)
    out_shape = (
        jax.ShapeDtypeStruct((b, s, POOL_WIDTH), F32),
        jax.ShapeDtypeStruct((b, NSA_WIDTH, s), BF16),
        jax.ShapeDtypeStruct((b, 2 * N_KV_GROUPS, s, HEAD_DIM), F32),
        jax.ShapeDtypeStruct((b, N_KV_GROUPS, s, HEAD_DIM), BF16),
        jax.ShapeDtypeStruct((b, N_KV_GROUPS, s // SLC_CHUNK, HEAD_DIM, SLC_CHUNK), BF16),
        jax.ShapeDtypeStruct((b, N_KV_GROUPS, s, HEAD_DIM), BF16),
        jax.ShapeDtypeStruct((b, N_KV_GROUPS, s // WIN_CHUNK, HEAD_DIM, WIN_CHUNK), BF16),
        jax.ShapeDtypeStruct((b, N_GATE_COLS, s), F32),
    )
    out_specs = (
        pl.BlockSpec((1, tm, POOL_WIDTH), tok),
        pl.BlockSpec((1, NSA_WIDTH, tm), lambda bi, m: (bi, 0, m)),
        pl.BlockSpec((1, 2 * N_KV_GROUPS, tm, HEAD_DIM), lambda bi, m: (bi, 0, m, 0)),
        pl.BlockSpec((1, N_KV_GROUPS, tm, HEAD_DIM), lambda bi, m: (bi, 0, m, 0)),
        pl.BlockSpec((1, N_KV_GROUPS, tm // SLC_CHUNK, HEAD_DIM, SLC_CHUNK), lambda bi, m: (bi, 0, m, 0, 0)),
        pl.BlockSpec((1, N_KV_GROUPS, tm, HEAD_DIM), lambda bi, m: (bi, 0, m, 0)),
        pl.BlockSpec((1, N_KV_GROUPS, tm // WIN_CHUNK, HEAD_DIM, WIN_CHUNK), lambda bi, m: (bi, 0, m, 0, 0)),
        pl.BlockSpec((1, N_GATE_COLS, tm), lambda bi, m: (bi, 0, m)),
    )
    return pl.pallas_call(
        _inproj_kernel,
        grid=(b, s // tm),
        in_specs=[
            pl.BlockSpec((1, tm, d), tok),
            pl.BlockSpec((1, d), const2),
            pl.BlockSpec((1, 1, d), per_b),
            pl.BlockSpec((1, 1, d), per_b),
            pl.BlockSpec((d, n_main), const2, pipeline_mode=pl.Buffered(1)),
            pl.BlockSpec((d, LANES), const2, pipeline_mode=pl.Buffered(1)),
        ],
        out_specs=out_specs,
        out_shape=out_shape,
        compiler_params=_params("arbitrary", "arbitrary"),
        name="input_projection",
    )(x, norm_g, scale, shift, w_main, w_gates)


def _pool_kernel(u_ref, pw_ref, ps_ref, o_ref, ext_ref):
    tm = u_ref.shape[1]
    halo = max(POOL_WINDOWS)
    m = pl.program_id(1)

    @pl.when(m == 0)
    def _():
        ext_ref[0:halo, :] = jnp.zeros((halo, POOL_WIDTH), F32)

    ext_ref[halo:halo + tm, :] = u_ref[0]
    t = m * tm + lax.broadcasted_iota(I32, (tm, 1), 0)
    for gi, w in enumerate(POOL_WINDOWS):
        cols = slice(gi * POOL_GROUP, (gi + 1) * POOL_GROUP)
        ug = ext_ref[halo:halo + tm, cols]
        acc = ug
        for i in range(1, w):
            acc = acc + ext_ref[halo - i:halo - i + tm, cols]
        cnt = jnp.minimum(t + 1, w).astype(F32)
        diff = (acc / cnt - ug).astype(BF16)
        o_ref[0, :, cols] = (_dot(diff, pw_ref[gi]) * ps_ref[:, cols]).astype(BF16)
    ext_ref[0:halo, :] = ext_ref[tm:tm + halo, :]


def _pool_mixer(u_pool, pool_w, pool_scale):
    b, s, _ = u_pool.shape
    tm = PROJ_TM
    return pl.pallas_call(
        _pool_kernel,
        grid=(b, s // tm),
        in_specs=[
            pl.BlockSpec((1, tm, POOL_WIDTH), lambda bi, m: (bi, m, 0)),
            pl.BlockSpec((len(POOL_WINDOWS), POOL_GROUP, POOL_GROUP), lambda bi, m: (0, 0, 0)),
            pl.BlockSpec((1, POOL_WIDTH), lambda bi, m: (0, 0)),
        ],
        out_specs=pl.BlockSpec((1, tm, POOL_WIDTH), lambda bi, m: (bi, m, 0)),
        out_shape=jax.ShapeDtypeStruct((b, s, POOL_WIDTH), BF16),
        scratch_shapes=[pltpu.VMEM((tm + max(POOL_WINDOWS), POOL_WIDTH), F32)],
        compiler_params=_params("arbitrary", "arbitrary"),
        name="pool_mixer",
    )(u_pool, pool_w, pool_scale)


def _compress_kernel(a_ref, pos_ref, w1_ref, b1_ref, w2_ref, b2_ref, o_ref, oT_ref):
    half = CMP_STRIDE * HEAD_DIM
    a = a_ref[0, 0]
    nb = a.shape[0]
    pos = pos_ref[0]
    a_lo = (a + pos[:, 0:half]).astype(BF16)
    a_hi = (a + pos[:, half:2 * half]).astype(BF16)
    p_lo = _dot(a_lo, w1_ref[0, 0:half, :])
    p_hi = _dot(a_hi, w1_ref[0, half:2 * half, :])
    hid = jax.nn.gelu(p_lo + pltpu.roll(p_hi, nb - 1, 0) + b1_ref[0])
    out = _dot(hid.astype(BF16), w2_ref[0]) + b2_ref[0]
    o_ref[0, 0] = out.astype(BF16)
    oT_ref[0, 0] = out.T.astype(BF16)


def _compress(kvc, pos, w1, b1, w2, b2):
    b, n4, s, _ = kvc.shape
    nb = s // CMP_STRIDE
    a = kvc.reshape(b, n4, nb, CMP_STRIDE * HEAD_DIM)
    hidden = w1.shape[-1]
    kv = lambda bi, i: (i // N_KV_GROUPS, 0, 0)
    return pl.pallas_call(
        _compress_kernel,
        grid=(b, n4),
        in_specs=[
            pl.BlockSpec((1, 1, nb, CMP_STRIDE * HEAD_DIM), lambda bi, i: (bi, i, 0, 0)),
            pl.BlockSpec((1, 1, CMP_BLOCK * HEAD_DIM), kv),
            pl.BlockSpec((1, CMP_BLOCK * HEAD_DIM, hidden), kv),
            pl.BlockSpec((1, 1, hidden), kv),
            pl.BlockSpec((1, hidden, HEAD_DIM), kv),
            pl.BlockSpec((1, 1, HEAD_DIM), kv),
        ],
        out_specs=(
            pl.BlockSpec((1, 1, nb, HEAD_DIM), lambda bi, i: (bi, i, 0, 0)),
            pl.BlockSpec((1, 1, HEAD_DIM, nb), lambda bi, i: (bi, i, 0, 0)),
        ),
        out_shape=(
            jax.ShapeDtypeStruct((b, n4, nb, HEAD_DIM), BF16),
            jax.ShapeDtypeStruct((b, n4, HEAD_DIM, nb), BF16),
        ),
        compiler_params=_params("arbitrary", "arbitrary"),
        name="kv_compress",
    )(a, pos, w1, b1, w2, b2)


def _flash_step(s, v_t, m_old, l_old, acc_ref):
    m_new = jnp.maximum(m_old, jnp.max(s, axis=0, keepdims=True))
    alpha = jnp.exp2(m_old - m_new)
    p = jnp.exp2(s - m_new)
    l_new = alpha * l_old + jnp.sum(p, axis=0, keepdims=True)
    acc_ref[...] = acc_ref[...] * alpha + _dot(v_t, p.astype(BF16))
    return m_new, l_new


def _nsa_kernel(qT_ref, kc_ref, vcT_ref, ksl_ref, vslT_ref, kwin_ref, vwinT_ref, gT_ref, y_ref,
                qcat_ref, pbuf_ref, score_ref, sel_ref, bias4_ref, sa_ref, sb_ref, ml_ref, wbuf_ref,
                ocmp_ref, aslc_ref, awin_ref):
    n = pl.program_id(2)
    q_start = n * Q_BLOCK
    nq = HEADS_PER_GROUP * Q_BLOCK
    nb = kc_ref.shape[2]
    ns = sel_ref.shape[0]

    for r in range(HEADS_PER_GROUP):
        qcat_ref[:, r * Q_BLOCK:(r + 1) * Q_BLOCK] = qT_ref[0, r * HEAD_DIM:(r + 1) * HEAD_DIM, :]
    qc = qcat_ref[...]
    tin = lax.broadcasted_iota(I32, (1, nq), 1) & (Q_BLOCK - 1)
    tok = q_start + tin

    s = _dot(kc_ref[0, 0], qc)
    ci = lax.broadcasted_iota(I32, (nb, nq), 0)
    s = jnp.where(ci * CMP_STRIDE + (CMP_BLOCK - 1) <= tok, s, NEG)
    m = jnp.max(s, axis=0, keepdims=True)
    e = jnp.exp2(s - m)
    l = jnp.sum(e, axis=0, keepdims=True)
    inv = jnp.where(tok >= CMP_BLOCK - 1, 1.0 / l, 0.0)
    p = e * inv
    ocmp_ref[...] = _dot(vcT_ref[0, 0], p.astype(BF16))

    psum = p[:, 0:Q_BLOCK]
    for r in range(1, HEADS_PER_GROUP):
        psum = psum + p[:, r * Q_BLOCK:(r + 1) * Q_BLOCK]
    pbuf_ref[0:SUBLANES, :] = jnp.zeros((SUBLANES, Q_BLOCK), F32)
    pbuf_ref[SUBLANES:SUBLANES + nb, :] = psum
    imp = pbuf_ref[pl.ds(SUBLANES - 1, ns, stride=SEL_RATIO), :]
    for d in range(SEL_RATIO):
        imp = imp + pbuf_ref[pl.ds(SUBLANES + d, ns, stride=SEL_RATIO), :]

    ji = lax.broadcasted_iota(I32, (ns, Q_BLOCK), 0)
    t1 = q_start + lax.broadcasted_iota(I32, (1, Q_BLOCK), 1)
    tj = t1 // SEL_BLOCK
    forced = (ji == 0) | (ji == tj) | (ji == tj - 1)
    score_ref[...] = jnp.where(forced, BIG, jnp.where(ji * SEL_BLOCK <= t1, imp, NEG))
    sel_ref[...] = jnp.full((ns, Q_BLOCK), NEG, F32)

    def pick(_, carry):
        sc = score_ref[...]
        mx = jnp.max(sc, axis=0, keepdims=True)
        first = jnp.min(jnp.where(sc == mx, ji, ns), axis=0, keepdims=True)
        hit = ji == first
        sel_ref[...] = jnp.where(hit & (mx >= 0.0), 0.0, sel_ref[...])
        score_ref[...] = jnp.where(hit, BELOW_NEG, sc)
        return carry

    lax.fori_loop(0, min(N_SELECT, ns), pick, 0)
    sel = sel_ref[...]
    for r in range(HEADS_PER_GROUP):
        bias4_ref[:, r * Q_BLOCK:(r + 1) * Q_BLOCK] = sel

    m0 = jnp.full((1, nq), NEG, F32)
    l0 = jnp.zeros((1, nq), F32)

    aslc_ref[...] = jnp.zeros(aslc_ref.shape, F32)
    blocks_per_chunk = SLC_CHUNK // SEL_BLOCK

    ml_ref[0:1, :] = m0
    ml_ref[1:2, :] = l0

    def slc_scores(c, buf_ref):
        k0 = pl.multiple_of(c * SLC_CHUNK, SLC_CHUNK)
        sc = _dot(ksl_ref[0, 0, pl.ds(k0, SLC_CHUNK), :], qc)
        for j in range(blocks_per_chunk):
            rows = slice(j * SEL_BLOCK, (j + 1) * SEL_BLOCK)
            buf_ref[rows, :] = sc[rows] + bias4_ref[pl.ds(c * blocks_per_chunk + j, 1), :]

    def slc_flash(c, buf_ref, diagonal):
        sc = buf_ref[...]
        if diagonal:
            kpos = c * SLC_CHUNK + lax.broadcasted_iota(I32, (SLC_CHUNK, nq), 0)
            sc = jnp.where(kpos <= tok, sc, NEG)
        m_new, l_new = _flash_step(sc, vslT_ref[0, 0, c], ml_ref[0:1, :], ml_ref[1:2, :], aslc_ref)
        ml_ref[0:1, :] = m_new
        ml_ref[1:2, :] = l_new

    def slc_pair(c):
        slc_scores(c + 1, sb_ref)
        slc_flash(c, sa_ref, False)
        slc_scores(c + 2, sa_ref)
        slc_flash(c + 1, sb_ref, False)

    def slc_quad(i, carry):
        slc_pair(4 * i)
        slc_pair(4 * i + 2)
        return carry

    n_past = q_start // SLC_CHUNK
    slc_scores(0, sa_ref)
    lax.fori_loop(0, n_past // 4, slc_quad, 0)
    c_pair = n_past & ~3

    @pl.when((n_past & 2) != 0)
    def _():
        slc_pair(c_pair)

    c_last = n_past & ~1

    @pl.when((n_past & 1) != 0)
    def _():
        slc_scores(c_last + 1, sb_ref)
        slc_flash(c_last, sa_ref, False)
        slc_flash(c_last + 1, sb_ref, True)

    @pl.when((n_past & 1) == 0)
    def _():
        slc_flash(c_last, sa_ref, True)

    l_slc = ml_ref[1:2, :]

    n_wc = WINDOW // WIN_CHUNK + 1
    kk = lax.broadcasted_iota(I32, (WIN_CHUNK, nq), 0)
    chunk_ids = []
    m_win = m0
    for i in range(n_wc):
        wb = n - (n_wc - 1) + i
        wbc = jnp.maximum(wb, 0)
        chunk_ids.append(wbc)
        k0 = pl.multiple_of(wbc * WIN_CHUNK, WIN_CHUNK)
        sc = _dot(kwin_ref[0, 0, pl.ds(k0, WIN_CHUNK), :], qc)
        if i == 0:
            sc = jnp.where(kk > tin, sc, NEG)
        if i == n_wc - 1:
            sc = jnp.where(kk <= tin, sc, NEG)
        else:
            sc = sc + jnp.where(wb >= 0, 0.0, NEG)
        wbuf_ref[i] = sc
        m_win = jnp.maximum(m_win, jnp.max(sc, axis=0, keepdims=True))
    l_win = l0
    for i in range(n_wc):
        p = jnp.exp2(wbuf_ref[i] - m_win)
        l_win = l_win + jnp.sum(p, axis=0, keepdims=True)
        pv = _dot(vwinT_ref[0, 0, chunk_ids[i]], p.astype(BF16))
        if i == 0:
            awin_ref[...] = pv
        else:
            awin_ref[...] = awin_ref[...] + pv

    gates = gT_ref[0, 0]
    inv_slc = 1.0 / l_slc
    inv_win = 1.0 / l_win
    for r in range(HEADS_PER_GROUP):
        cols = slice(r * Q_BLOCK, (r + 1) * Q_BLOCK)
        o = (gates[3 * r:3 * r + 1] * ocmp_ref[:, cols]
             + gates[3 * r + 1:3 * r + 2] * inv_slc[:, cols] * aslc_ref[:, cols]
             + gates[3 * r + 2:3 * r + 3] * inv_win[:, cols] * awin_ref[:, cols])
        y_ref[0, :, r * HEAD_DIM:(r + 1) * HEAD_DIM] = o.T.astype(BF16)


def _nsa(qT, kc, kcT, ksl, vslT, kwin, vwinT, gT):
    b, _, s = qT.shape
    nb = kc.shape[2]
    ns = s // SEL_BLOCK
    g4 = HEADS_PER_GROUP * HEAD_DIM
    nq = HEADS_PER_GROUP * Q_BLOCK
    gates = gT.reshape(b, N_KV_GROUPS, HEADS_PER_GROUP * 3, s)
    per_bg = lambda bi, g, n: (bi, g, 0, 0)
    per_bg5 = lambda bi, g, n: (bi, g, 0, 0, 0)
    return pl.pallas_call(
        _nsa_kernel,
        grid=(b, N_KV_GROUPS, s // Q_BLOCK),
        in_specs=[
            pl.BlockSpec((1, g4, Q_BLOCK), lambda bi, g, n: (bi, g, n)),
            pl.BlockSpec((1, 1, nb, HEAD_DIM), per_bg),
            pl.BlockSpec((1, 1, HEAD_DIM, nb), lambda bi, g, n: (bi, N_KV_GROUPS + g, 0, 0)),
            pl.BlockSpec((1, 1, s, HEAD_DIM), per_bg),
            pl.BlockSpec((1, 1, s // SLC_CHUNK, HEAD_DIM, SLC_CHUNK), per_bg5),
            pl.BlockSpec((1, 1, s, HEAD_DIM), per_bg),
            pl.BlockSpec((1, 1, s // WIN_CHUNK, HEAD_DIM, WIN_CHUNK), per_bg5),
            pl.BlockSpec((1, 1, HEADS_PER_GROUP * 3, Q_BLOCK), lambda bi, g, n: (bi, g, 0, n)),
        ],
        out_specs=pl.BlockSpec((1, Q_BLOCK, g4), lambda bi, g, n: (bi, n, g)),
        out_shape=jax.ShapeDtypeStruct((b, s, NSA_WIDTH), BF16),
        scratch_shapes=[
            pltpu.VMEM((HEAD_DIM, nq), BF16),
            pltpu.VMEM((nb + SUBLANES, Q_BLOCK), F32),
            pltpu.VMEM((ns, Q_BLOCK), F32),
            pltpu.VMEM((ns, Q_BLOCK), F32),
            pltpu.VMEM((ns, nq), F32),
            pltpu.VMEM((SLC_CHUNK, nq), F32),
            pltpu.VMEM((SLC_CHUNK, nq), F32),
            pltpu.VMEM((SUBLANES, nq), F32),
            pltpu.VMEM((WINDOW // WIN_CHUNK + 1, WIN_CHUNK, nq), F32),
            pltpu.VMEM((HEAD_DIM, nq), F32),
            pltpu.VMEM((HEAD_DIM, nq), F32),
            pltpu.VMEM((HEAD_DIM, nq), F32),
        ],
        compiler_params=_params("arbitrary", "arbitrary", "arbitrary"),
        name="nsa_attention",
    )(qT, kc, kcT, ksl, vslT, kwin, vwinT, gates)


def _outproj_kernel(yp_ref, yn_ref, wo_ref, x_ref, g1_ref, n2_ref, sc_ref, sh_ref, wrT_ref, br_ref, tri_ref,
                    x1_ref, h2_ref, idx_ref, wt_ref, rank_ref, cnt_ref, carry_ref):
    tm = x_ref.shape[1]
    n_exp = wrT_ref.shape[0]

    @pl.when((pl.program_id(0) == 0) & (pl.program_id(1) == 0))
    def _():
        carry_ref[...] = jnp.zeros(carry_ref.shape, F32)

    mix = _dot(yp_ref[0], wo_ref[0:POOL_WIDTH, :]) + _dot(yn_ref[0], wo_ref[POOL_WIDTH:POOL_WIDTH + NSA_WIDTH, :])
    x1 = x_ref[0] + g1_ref[0] * mix
    x1_ref[0] = x1
    h2 = _rmsnorm(x1, n2_ref[...]) * (1.0 + sc_ref[0]) + sh_ref[0]
    h2_ref[0] = h2

    logits = lax.dot_general(wrT_ref[...], h2.astype(BF16), (((1,), (1,)), ((), ())),
                             preferred_element_type=F32) + br_ref[...]
    ei = lax.broadcasted_iota(I32, (n_exp, tm), 0)
    vals, ids = [], []
    for _ in range(TOP_K):
        mx = jnp.max(logits, axis=0, keepdims=True)
        first = jnp.min(jnp.where(logits == mx, ei, n_exp), axis=0, keepdims=True)
        vals.append(mx)
        ids.append(first)
        logits = jnp.where(ei == first, -jnp.inf, logits)
    exps = [jnp.exp(v - vals[0]) for v in vals]
    den = exps[0]
    for e in exps[1:]:
        den = den + e

    onehot = jnp.zeros((n_exp, tm), F32)
    for first in ids:
        onehot = onehot + (ei == first).astype(F32)
    before = _dot(onehot.astype(BF16), tri_ref[...]) + carry_ref[:, 0:1]
    wt_ref[TOP_K:SUBLANES, :] = jnp.zeros((SUBLANES - TOP_K, tm), F32)
    for k in range(TOP_K):
        idx_ref[k:k + 1, :] = ids[k]
        wt_ref[k:k + 1, :] = exps[k] / den
        rank = jnp.sum(jnp.where(ei == ids[k], before, 0.0), axis=0, keepdims=True)
        rank_ref[k:k + 1, :] = rank.astype(I32)
    carry_ref[...] = carry_ref[...] + jnp.sum(onehot, axis=1, keepdims=True)
    cnt_ref[...] = carry_ref[...].astype(I32)


def _output_projection(y_pool, y_nsa, w_out, x, gate1, norm_g, scale, shift, w_router_t, b_router, tri):
    b, s, d = x.shape
    tm = PROJ_TM
    n_exp = w_router_t.shape[0]
    t_all = b * s
    tok = lambda bi, m: (bi, m, 0)
    per_b = lambda bi, m: (bi, 0, 0)
    const2 = lambda bi, m: (0, 0)
    flat = lambda bi, m: (0, bi * (s // tm) + m)
    return pl.pallas_call(
        _outproj_kernel,
        grid=(b, s // tm),
        in_specs=[
            pl.BlockSpec((1, tm, POOL_WIDTH), tok),
            pl.BlockSpec((1, tm, NSA_WIDTH), tok),
            pl.BlockSpec((POOL_WIDTH + NSA_WIDTH, d), const2, pipeline_mode=pl.Buffered(1)),
            pl.BlockSpec((1, tm, d), tok),
            pl.BlockSpec((1, 1, d), per_b),
            pl.BlockSpec((1, d), const2),
            pl.BlockSpec((1, 1, d), per_b),
            pl.BlockSpec((1, 1, d), per_b),
            pl.BlockSpec((n_exp, d), const2),
            pl.BlockSpec((n_exp, 1), const2),
            pl.BlockSpec((tm, tm), const2, pipeline_mode=pl.Buffered(1)),
        ],
        out_specs=(
            pl.BlockSpec((1, tm, d), tok),
            pl.BlockSpec((1, tm, d), tok),
            pl.BlockSpec((TOP_K, tm), flat),
            pl.BlockSpec((SUBLANES, tm), flat),
            pl.BlockSpec((TOP_K, tm), flat),
            pl.BlockSpec((n_exp, LANES), const2),
        ),
        out_shape=(
            jax.ShapeDtypeStruct((b, s, d), F32),
            jax.ShapeDtypeStruct((b, s, d), F32),
            jax.ShapeDtypeStruct((TOP_K, t_all), I32),
            jax.ShapeDtypeStruct((SUBLANES, t_all), F32),
            jax.ShapeDtypeStruct((TOP_K, t_all), I32),
            jax.ShapeDtypeStruct((n_exp, LANES), I32),
        ),
        scratch_shapes=[pltpu.VMEM((n_exp, LANES), F32)],
        compiler_params=_params("arbitrary", "arbitrary"),
        name="output_projection_router",
    )(y_pool, y_nsa, w_out, x, gate1, norm_g, scale, shift, w_router_t, b_router, tri)


def _dest_kernel(ps_ref, idx_ref, rank_ref, dest_ref):
    idx = idx_ref[...]
    dest = rank_ref[...]
    for e in range(ps_ref.shape[0]):
        dest = dest + jnp.where(idx == e, ps_ref[e], 0)
    dest_ref[...] = dest


def _destinations(seg_starts, idx_t, rank_t):
    return pl.pallas_call(
        _dest_kernel,
        in_specs=[
            pl.BlockSpec(memory_space=pltpu.SMEM),
            pl.BlockSpec(memory_space=pltpu.VMEM),
            pl.BlockSpec(memory_space=pltpu.VMEM),
        ],
        out_specs=pl.BlockSpec(memory_space=pltpu.VMEM),
        out_shape=jax.ShapeDtypeStruct(idx_t.shape, I32),
        name="moe_destinations",
    )(seg_starts, idx_t, rank_t)


def _row_copy(src_ref, src_row, dst_ref, dst_row, sem):
    return pltpu.make_async_copy(src_ref.at[pl.ds(src_row, 1)], dst_ref.at[pl.ds(dst_row, 1)], sem)


def _dispatch_kernel(dest_ref, starts_ref, ends_ref, h_ref, xs_ref, zero_ref, sem, zsem):
    tm = h_ref.shape[0]
    n_exp = starts_ref.shape[0]
    zrows = zero_ref.shape[0]

    @pl.when(pl.program_id(0) == 0)
    def _():
        zero_ref[...] = jnp.zeros(zero_ref.shape, zero_ref.dtype)

        def clear_block(end_row, start):
            for r in range(EXPERT_TM // zrows):
                row0 = pl.multiple_of(end_row - EXPERT_TM + r * zrows, zrows)
                cp = pltpu.make_async_copy(zero_ref, xs_ref.at[pl.ds(row0, zrows)], zsem)
                cp.start() if start else cp.wait()

        def clear(e, start):
            @pl.when(ends_ref[e] > starts_ref[e])
            def _():
                clear_block(ends_ref[e], start)

        first_unused = ends_ref[n_exp - 1] // EXPERT_TM
        n_blk = xs_ref.shape[0] // EXPERT_TM
        for start in (True, False):
            lax.fori_loop(0, n_exp, lambda e, c: (clear(e, start), c)[1], 0)
            lax.fori_loop(first_unused, n_blk, lambda i, c: (clear_block((i + 1) * EXPERT_TM, start), c)[1], 0)

    def issue(t, carry):
        for k in range(TOP_K):
            _row_copy(h_ref, t, xs_ref, dest_ref[k, t], sem).start()
        return carry

    lax.fori_loop(0, tm, issue, 0)

    def drain(t, carry):
        for k in range(TOP_K):
            _row_copy(h_ref, t, xs_ref, dest_ref[k, t], sem).wait()
        return carry

    lax.fori_loop(0, tm, drain, 0)


def _dispatch(dest_t, seg_starts, seg_ends, h_rows, n_buf):
    t_all, d = h_rows.shape
    tm = DISPATCH_TM
    return pl.pallas_call(
        _dispatch_kernel,
        grid=(t_all // tm,),
        in_specs=[
            pl.BlockSpec((TOP_K, tm), lambda i: (0, i), memory_space=pltpu.SMEM),
            pl.BlockSpec(memory_space=pltpu.SMEM),
            pl.BlockSpec(memory_space=pltpu.SMEM),
            pl.BlockSpec((tm, d), lambda i: (i, 0)),
        ],
        out_specs=pl.BlockSpec(memory_space=pl.ANY),
        out_shape=jax.ShapeDtypeStruct((n_buf, d), h_rows.dtype),
        scratch_shapes=[pltpu.VMEM((DISPATCH_ZERO_ROWS, d), h_rows.dtype),
                        pltpu.SemaphoreType.DMA(()), pltpu.SemaphoreType.DMA(())],
        compiler_params=_params("arbitrary"),
        name="moe_dispatch",
    )(dest_t, seg_starts, seg_ends, h_rows)


def _ffn_up_kernel(be_ref, nu_ref, xs_ref, wg_ref, bg_ref, wu_ref, bu_ref, h_ref, xb_ref):
    del be_ref
    i = pl.program_id(0)
    j = pl.program_id(1)
    used = i < nu_ref[0]

    @pl.when(used & (j == 0))
    def _():
        xb_ref[...] = xs_ref[...].astype(BF16)

    @pl.when(used)
    def _():
        xb = xb_ref[...]
        gl = jnp.minimum(_dot(xb, wg_ref[0, 0].astype(BF16)) + bg_ref[0, 0], SWIGLU_LIMIT)
        lin = jnp.clip(_dot(xb, wu_ref[0, 0].astype(BF16)) + bu_ref[0, 0], -SWIGLU_LIMIT, SWIGLU_LIMIT)
        h_ref[...] = (gl * jax.nn.sigmoid(SWIGLU_ALPHA * gl) * (lin + 1.0)).astype(BF16)

    @pl.when(jnp.logical_not(used))
    def _():
        h_ref[...] = jnp.zeros(h_ref.shape, BF16)


def _ffn_down_kernel(be_ref, nu_ref, h_ref, wd_ref, bd_ref, y_ref):
    del be_ref
    used = pl.program_id(0) < nu_ref[0]

    @pl.when(used)
    def _():
        y_ref[...] = _dot(h_ref[...], wd_ref[0, 0].astype(BF16)) + bd_ref[0, 0]

    @pl.when(jnp.logical_not(used))
    def _():
        y_ref[...] = jnp.zeros(y_ref.shape, F32)


def _expert_ffn(layer, block_expert, n_used, xs, w_gate, b_gate, w_up, b_up, w_down, b_down):
    n_buf, d = xs.shape
    depth, n_exp, _, f = w_gate.shape
    tm, tf = EXPERT_TM, EXPERT_TF
    nf = f // tf
    nd = d // tf
    n_blk = n_buf // tm

    def rows(i, j, be, nu):
        return (jnp.minimum(i, nu[0] - 1), 0)

    def cols(nj):
        return lambda i, j, be, nu: (layer, be[i], 0, jnp.where(i < nu[0], j, nj - 1))

    hidden = pl.pallas_call(
        _ffn_up_kernel,
        grid_spec=pltpu.PrefetchScalarGridSpec(
            num_scalar_prefetch=2,
            grid=(n_blk, nf),
            in_specs=[
                pl.BlockSpec((tm, d), rows),
                pl.BlockSpec((1, 1, d, tf), cols(nf)),
                pl.BlockSpec((1, 1, 1, tf), cols(nf)),
                pl.BlockSpec((1, 1, d, tf), cols(nf)),
                pl.BlockSpec((1, 1, 1, tf), cols(nf)),
            ],
            out_specs=pl.BlockSpec((tm, tf), lambda i, j, be, nu: (i, j)),
            scratch_shapes=[pltpu.VMEM((tm, d), BF16)],
        ),
        out_shape=jax.ShapeDtypeStruct((n_buf, f), BF16),
        compiler_params=_params("arbitrary", "arbitrary"),
        name="moe_ffn_up",
    )(block_expert, n_used, xs, w_gate, b_gate.reshape(depth, n_exp, 1, f), w_up, b_up.reshape(depth, n_exp, 1, f))
    return pl.pallas_call(
        _ffn_down_kernel,
        grid_spec=pltpu.PrefetchScalarGridSpec(
            num_scalar_prefetch=2,
            grid=(n_blk, nd),
            in_specs=[
                pl.BlockSpec((tm, f), rows),
                pl.BlockSpec((1, 1, f, tf), cols(nd)),
                pl.BlockSpec((1, 1, 1, tf), cols(nd)),
            ],
            out_specs=pl.BlockSpec((tm, tf), lambda i, j, be, nu: (i, j)),
        ),
        out_shape=jax.ShapeDtypeStruct((n_buf, d), F32),
        compiler_params=_params("arbitrary", "arbitrary"),
        name="moe_ffn_down",
    )(block_expert, n_used, hidden, w_down, b_down.reshape(depth, n_exp, 1, d))


def _combine_kernel(dest_ref, wt_ref, x1_ref, g2_ref, fg_ref, y_ref, o_ref, rows_ref, sem, *, final_norm):
    tm = x1_ref.shape[1]

    def issue(t, carry):
        for k in range(TOP_K):
            _row_copy(y_ref, dest_ref[k, t], rows_ref.at[k], t, sem).start()
        return carry

    lax.fori_loop(0, tm, issue, 0)

    wt = jnp.concatenate([wt_ref[...], jnp.zeros((tm - SUBLANES, tm), F32)], axis=0).T

    def drain(t, carry):
        for k in range(TOP_K):
            _row_copy(y_ref, dest_ref[k, t], rows_ref.at[k], t, sem).wait()
        return carry

    lax.fori_loop(0, tm, drain, 0)

    acc = wt[:, 0:1] * rows_ref[0]
    for k in range(1, TOP_K):
        acc = acc + wt[:, k:k + 1] * rows_ref[k]
    out = x1_ref[0] + g2_ref[0] * acc
    if final_norm:
        out = _rmsnorm(out, fg_ref[...])
    o_ref[0] = out


def _combine(dest_t, wt_t, x1, gate2, final_g, y_rows, final_norm):
    b, s, d = x1.shape
    tm = COMBINE_TM
    flat = lambda bi, m: (0, bi * (s // tm) + m)
    return pl.pallas_call(
        functools.partial(_combine_kernel, final_norm=final_norm),
        grid=(b, s // tm),
        in_specs=[
            pl.BlockSpec((TOP_K, tm), flat, memory_space=pltpu.SMEM),
            pl.BlockSpec((SUBLANES, tm), flat),
            pl.BlockSpec((1, tm, d), lambda bi, m: (bi, m, 0)),
            pl.BlockSpec((1, 1, d), lambda bi, m: (bi, 0, 0)),
            pl.BlockSpec((1, d), lambda bi, m: (0, 0)),
            pl.BlockSpec(memory_space=pl.ANY),
        ],
        out_specs=pl.BlockSpec((1, tm, d), lambda bi, m: (bi, m, 0)),
        out_shape=jax.ShapeDtypeStruct((b, s, d), F32),
        scratch_shapes=[pltpu.VMEM((TOP_K, tm, d), F32), pltpu.SemaphoreType.DMA(())],
        compiler_params=_params("arbitrary", "arbitrary"),
        name="moe_combine",
    )(dest_t, wt_t, x1, gate2, final_g, y_rows)


def _segment_layout(counts, n_blk):
    padded = (counts + EXPERT_TM - 1) // EXPERT_TM * EXPERT_TM
    ends = jnp.cumsum(padded)
    starts = ends - padded
    n_used = ends[-1] // EXPERT_TM
    blk = jnp.minimum(jnp.arange(n_blk, dtype=I32), n_used - 1) * EXPERT_TM
    block_expert = jnp.minimum(jnp.searchsorted(ends, blk, side="right"), counts.shape[0] - 1)
    return starts.astype(I32), ends.astype(I32), block_expert.astype(I32), n_used.reshape(1).astype(I32)


def kernel(x, c, norm1_g, norm2_g, w_ada, b_ada, w_in, pool_w, pool_scale, cmp_pos_k, cmp_w1_k, cmp_b1_k, cmp_w2_k, cmp_b2_k, cmp_pos_v, cmp_w1_v, cmp_b1_v, cmp_w2_v, cmp_b2_v, w_out, w_router, b_router, w_gate, b_gate, w_up, b_up, w_down, b_down, final_g):
    b, s, d = x.shape
    depth = w_ada.shape[0]
    n_exp = w_router.shape[-1]
    t_all = b * s
    assert s % PROJ_TM == 0 and s % SLC_CHUNK == 0 and t_all % DISPATCH_TM == 0
    n_main = POOL_WIDTH + NSA_WIDTH + 6 * KV_WIDTH
    assert w_in.shape[-1] == n_main + N_GATE_COLS

    c_pad = jnp.pad(c, ((0, SUBLANES - b), (0, 0)))
    mod = _modulation(c_pad, w_ada, b_ada)[:, :b]
    tri = (jnp.arange(PROJ_TM)[:, None] < jnp.arange(PROJ_TM)[None, :]).astype(BF16)
    n_buf = (t_all * TOP_K + n_exp * (EXPERT_TM - 1)) // EXPERT_TM * EXPERT_TM
    n_blk = n_buf // EXPERT_TM

    for l in range(depth):
        sh1, sc1, g1, sh2, sc2, g2 = [m.reshape(b, 1, d) for m in jnp.split(mod[l], 6, axis=-1)]
        w_main = w_in[l, :, :n_main].astype(BF16)
        w_gates = jnp.pad(w_in[l, :, n_main:], ((0, 0), (0, LANES - N_GATE_COLS))).astype(BF16)
        u_pool, qT, kvc, ksl, vslT, kwin, vwinT, gT = _input_projection(
            x, norm1_g[l].reshape(1, d), sc1, sh1, w_main, w_gates)
        y_pool = _pool_mixer(u_pool, pool_w[l].astype(BF16), pool_scale[l].reshape(1, POOL_WIDTH))
        kc, kcT = _compress(
            kvc,
            jnp.stack([cmp_pos_k[l], cmp_pos_v[l]]).reshape(2, 1, CMP_BLOCK * HEAD_DIM),
            jnp.stack([cmp_w1_k[l], cmp_w1_v[l]]).astype(BF16),
            jnp.stack([cmp_b1_k[l], cmp_b1_v[l]])[:, None, :],
            jnp.stack([cmp_w2_k[l], cmp_w2_v[l]]).astype(BF16),
            jnp.stack([cmp_b2_k[l], cmp_b2_v[l]])[:, None, :])
        y_nsa = _nsa(qT, kc, kcT, ksl, vslT, kwin, vwinT, gT)
        x1, h2, idx_t, wt_t, rank_t, counts = _output_projection(
            y_pool, y_nsa, w_out[l].astype(BF16), x, g1, norm2_g[l].reshape(1, d), sc2, sh2,
            w_router[l].T.astype(BF16), b_router[l].reshape(n_exp, 1), tri)
        seg_starts, seg_ends, block_expert, n_used = _segment_layout(counts[:, 0], n_blk)
        dest_t = _destinations(seg_starts, idx_t, rank_t)
        xs = _dispatch(dest_t, seg_starts, seg_ends, h2.reshape(t_all, d), n_buf)
        y_rows = _expert_ffn(l, block_expert, n_used, xs, w_gate, b_gate, w_up, b_up, w_down, b_down)
        x = _combine(dest_t, wt_t, x1, g2, final_g.reshape(1, d), y_rows, final_norm=(l == depth - 1))
    return x
```

```python
import functools

import jax
import jax.numpy as jnp
from jax import lax
from jax.experimental import pallas as pl
from jax.experimental.pallas import tpu as pltpu

F32 = jnp.float32
BF16 = jnp.bfloat16
I32 = jnp.int32

HEAD_DIM = 128
N_HEADS = 8
N_KV_GROUPS = 2
HEADS_PER_GROUP = N_HEADS // N_KV_GROUPS
POOL_WINDOWS = (2, 4, 8, 16)
POOL_GROUP = 256
POOL_WIDTH = POOL_GROUP * len(POOL_WINDOWS)
NSA_WIDTH = N_HEADS * HEAD_DIM
KV_WIDTH = N_KV_GROUPS * HEAD_DIM
N_GATE_COLS = 3 * N_HEADS
CMP_BLOCK = 32
CMP_STRIDE = 16
SEL_BLOCK = 64
SEL_RATIO = SEL_BLOCK // CMP_STRIDE
N_SELECT = 16
WINDOW = 512
Q_BLOCK = 128
TOP_K = 4
SWIGLU_ALPHA = 1.702
SWIGLU_LIMIT = 7.0
NORM_EPS = 1e-5
NEG = -1e30
BIG = 1e30
BELOW_NEG = -3e38
V_ROWS = HEAD_DIM + 16
LOG2_E = 1.4426950408889634

LANES = 128
SUBLANES = 8
VMEM_LIMIT_BYTES = 56 * 1024 * 1024

MOD_TN = 1024
PROJ_TM = 512
SLC_CHUNK = 256
WIN_CHUNK = 128
DISPATCH_TM = 256
DISPATCH_ZERO_ROWS = 256
COMBINE_TM = 128
EXPERT_TM = 1024
EXPERT_TF = 512


def _params(*sem):
    return pltpu.CompilerParams(dimension_semantics=sem, vmem_limit_bytes=VMEM_LIMIT_BYTES)


def _dot(a, b):
    return jnp.dot(a, b, preferred_element_type=F32)


def _rmsnorm(x, g):
    return x * lax.rsqrt(jnp.mean(x * x, axis=-1, keepdims=True) + NORM_EPS) * g


def _mod_kernel(c_ref, w_ref, b_ref, o_ref):
    c = c_ref[...]
    ca = (c * jax.nn.sigmoid(c)).astype(BF16)
    o_ref[0] = _dot(ca, w_ref[0].astype(BF16)) + b_ref[0]


def _modulation(c_pad, w_ada, b_ada):
    depth, d, n = w_ada.shape
    rows = c_pad.shape[0]
    return pl.pallas_call(
        _mod_kernel,
        grid=(depth, n // MOD_TN),
        in_specs=[
            pl.BlockSpec((rows, d), lambda l, j: (0, 0)),
            pl.BlockSpec((1, d, MOD_TN), lambda l, j: (l, 0, j)),
            pl.BlockSpec((1, 1, MOD_TN), lambda l, j: (l, 0, j)),
        ],
        out_specs=pl.BlockSpec((1, rows, MOD_TN), lambda l, j: (l, 0, j)),
        out_shape=jax.ShapeDtypeStruct((depth, rows, n), F32),
        compiler_params=_params("arbitrary", "arbitrary"),
        name="adaln_modulation",
    )(c_pad, w_ada, b_ada.reshape(depth, 1, n))


def _inproj_kernel(x_ref, g_ref, sc_ref, sh_ref, w_ref, wg_ref,
                   upool_ref, qT_ref, kvc_ref, ksl_ref, vslT_ref, kwin_ref, vwinT_ref, gT_ref):
    tm = x_ref.shape[1]
    h = _rmsnorm(x_ref[0], g_ref[...]) * (1.0 + sc_ref[0]) + sh_ref[0]
    hb = h.astype(BF16)
    upool_ref[0] = _dot(hb, w_ref[:, 0:POOL_WIDTH])
    q0 = POOL_WIDTH
    scale = HEAD_DIM ** -0.5 * LOG2_E
    def four_heads(col0):
        wide = _dot(hb, w_ref[:, col0:col0 + 4 * HEAD_DIM])
        return [wide[:, i * HEAD_DIM:(i + 1) * HEAD_DIM] for i in range(4)]

    for j in range(N_HEADS // 4):
        for i, q in enumerate(four_heads(q0 + j * 4 * HEAD_DIM)):
            hd = 4 * j + i
            qT_ref[0, hd * HEAD_DIM:(hd + 1) * HEAD_DIM, :] = (q * scale).T.astype(BF16)
    c0 = q0 + NSA_WIDTH
    for i, kv in enumerate(four_heads(c0)):
        kvc_ref[0, i] = kv
    s0 = c0 + 2 * KV_WIDTH
    w0 = s0 + 2 * KV_WIDTH
    slc = four_heads(s0)
    win = four_heads(w0)
    tpos = pl.program_id(1) * tm + lax.broadcasted_iota(I32, (tm, HEAD_DIM), 0)
    lane = lax.broadcasted_iota(I32, (tm, HEAD_DIM), 1)
    block_onehot = jnp.where(lane == ((tpos // SEL_BLOCK) & (LANES - 1)), 1.0, 0.0).astype(BF16)
    for g in range(N_KV_GROUPS):
        ksl_ref[0, g, :, 0:HEAD_DIM] = slc[g].astype(BF16)
        ksl_ref[0, g, :, HEAD_DIM:2 * HEAD_DIM] = block_onehot
        vT = slc[N_KV_GROUPS + g].T.astype(BF16)
        for c in range(tm // SLC_CHUNK):
            vslT_ref[0, g, c, 0:HEAD_DIM, :] = vT[:, c * SLC_CHUNK:(c + 1) * SLC_CHUNK]
            vslT_ref[0, g, c, HEAD_DIM:V_ROWS, :] = jnp.ones((V_ROWS - HEAD_DIM, SLC_CHUNK), BF16)
        kwin_ref[0, g] = win[g].astype(BF16)
        vT = win[N_KV_GROUPS + g].T.astype(BF16)
        for c in range(tm // WIN_CHUNK):
            vwinT_ref[0, g, c, 0:HEAD_DIM, :] = vT[:, c * WIN_CHUNK:(c + 1) * WIN_CHUNK]
            vwinT_ref[0, g, c, HEAD_DIM:V_ROWS, :] = jnp.ones((V_ROWS - HEAD_DIM, WIN_CHUNK), BF16)
    gl = jax.nn.sigmoid(_dot(hb, wg_ref[...]))
    gT_ref[0] = gl.T[0:N_GATE_COLS, :]


def _input_projection(x, norm_g, scale, shift, w_main, w_gates):
    b, s, d = x.shape
    tm = PROJ_TM
    n_main = w_main.shape[1]
    tok = lambda bi, m: (bi, m, 0)
    per_b = lambda bi, m: (bi, 0, 0)
    const2 = lambda bi, m: (0, 0)
    out_shape = (
        jax.ShapeDtypeStruct((b, s, POOL_WIDTH), F32),
        jax.ShapeDtypeStruct((b, NSA_WIDTH, s), BF16),
        jax.ShapeDtypeStruct((b, 2 * N_KV_GROUPS, s, HEAD_DIM), F32),
        jax.ShapeDtypeStruct((b, N_KV_GROUPS, s, 2 * HEAD_DIM), BF16),
        jax.ShapeDtypeStruct((b, N_KV_GROUPS, s // SLC_CHUNK, V_ROWS, SLC_CHUNK), BF16),
        jax.ShapeDtypeStruct((b, N_KV_GROUPS, s, HEAD_DIM), BF16),
        jax.ShapeDtypeStruct((b, N_KV_GROUPS, s // WIN_CHUNK, V_ROWS, WIN_CHUNK), BF16),
        jax.ShapeDtypeStruct((b, N_GATE_COLS, s), F32),
    )
    out_specs = (
        pl.BlockSpec((1, tm, POOL_WIDTH), tok),
        pl.BlockSpec((1, NSA_WIDTH, tm), lambda bi, m: (bi, 0, m)),
        pl.BlockSpec((1, 2 * N_KV_GROUPS, tm, HEAD_DIM), lambda bi, m: (bi, 0, m, 0)),
        pl.BlockSpec((1, N_KV_GROUPS, tm, 2 * HEAD_DIM), lambda bi, m: (bi, 0, m, 0)),
        pl.BlockSpec((1, N_KV_GROUPS, tm // SLC_CHUNK, V_ROWS, SLC_CHUNK), lambda bi, m: (bi, 0, m, 0, 0)),
        pl.BlockSpec((1, N_KV_GROUPS, tm, HEAD_DIM), lambda bi, m: (bi, 0, m, 0)),
        pl.BlockSpec((1, N_KV_GROUPS, tm // WIN_CHUNK, V_ROWS, WIN_CHUNK), lambda bi, m: (bi, 0, m, 0, 0)),
        pl.BlockSpec((1, N_GATE_COLS, tm), lambda bi, m: (bi, 0, m)),
    )
    return pl.pallas_call(
        _inproj_kernel,
        grid=(b, s // tm),
        in_specs=[
            pl.BlockSpec((1, tm, d), tok),
            pl.BlockSpec((1, d), const2),
            pl.BlockSpec((1, 1, d), per_b),
            pl.BlockSpec((1, 1, d), per_b),
            pl.BlockSpec((d, n_main), const2, pipeline_mode=pl.Buffered(1)),
            pl.BlockSpec((d, LANES), const2, pipeline_mode=pl.Buffered(1)),
        ],
        out_specs=out_specs,
        out_shape=out_shape,
        compiler_params=_params("arbitrary", "arbitrary"),
        name="input_projection",
    )(x, norm_g, scale, shift, w_main, w_gates)


def _pool_kernel(u_ref, pw_ref, ps_ref, o_ref, ext_ref):
    tm = u_ref.shape[1]
    halo = max(POOL_WINDOWS)
    m = pl.program_id(1)

    @pl.when(m == 0)
    def _():
        ext_ref[0:halo, :] = jnp.zeros((halo, POOL_WIDTH), F32)

    ext_ref[halo:halo + tm, :] = u_ref[0]
    t = m * tm + lax.broadcasted_iota(I32, (tm, 1), 0)
    for gi, w in enumerate(POOL_WINDOWS):
        cols = slice(gi * POOL_GROUP, (gi + 1) * POOL_GROUP)
        ug = ext_ref[halo:halo + tm, cols]
        acc = ug
        for i in range(1, w):
            acc = acc + ext_ref[halo - i:halo - i + tm, cols]
        cnt = jnp.minimum(t + 1, w).astype(F32)
        diff = (acc / cnt - ug).astype(BF16)
        o_ref[0, :, cols] = (_dot(diff, pw_ref[gi]) * ps_ref[:, cols]).astype(BF16)
    ext_ref[0:halo, :] = ext_ref[tm:tm + halo, :]


def _pool_mixer(u_pool, pool_w, pool_scale):
    b, s, _ = u_pool.shape
    tm = PROJ_TM
    return pl.pallas_call(
        _pool_kernel,
        grid=(b, s // tm),
        in_specs=[
            pl.BlockSpec((1, tm, POOL_WIDTH), lambda bi, m: (bi, m, 0)),
            pl.BlockSpec((len(POOL_WINDOWS), POOL_GROUP, POOL_GROUP), lambda bi, m: (0, 0, 0)),
            pl.BlockSpec((1, POOL_WIDTH), lambda bi, m: (0, 0)),
        ],
        out_specs=pl.BlockSpec((1, tm, POOL_WIDTH), lambda bi, m: (bi, m, 0)),
        out_shape=jax.ShapeDtypeStruct((b, s, POOL_WIDTH), BF16),
        scratch_shapes=[pltpu.VMEM((tm + max(POOL_WINDOWS), POOL_WIDTH), F32)],
        compiler_params=_params("arbitrary", "arbitrary"),
        name="pool_mixer",
    )(u_pool, pool_w, pool_scale)


def _compress_kernel(a_ref, pos_ref, w1_ref, b1_ref, w2_ref, b2_ref, o_ref, oT_ref):
    half = CMP_STRIDE * HEAD_DIM
    a = a_ref[0, 0]
    nb = a.shape[0]
    pos = pos_ref[0]
    a_lo = (a + pos[:, 0:half]).astype(BF16)
    a_hi = (a + pos[:, half:2 * half]).astype(BF16)
    p_lo = _dot(a_lo, w1_ref[0, 0:half, :])
    p_hi = _dot(a_hi, w1_ref[0, half:2 * half, :])
    hid = jax.nn.gelu(p_lo + pltpu.roll(p_hi, nb - 1, 0) + b1_ref[0])
    out = _dot(hid.astype(BF16), w2_ref[0]) + b2_ref[0]
    o_ref[0, 0] = out.astype(BF16)
    oT_ref[0, 0] = out.T.astype(BF16)


def _compress(kvc, pos, w1, b1, w2, b2):
    b, n4, s, _ = kvc.shape
    nb = s // CMP_STRIDE
    a = kvc.reshape(b, n4, nb, CMP_STRIDE * HEAD_DIM)
    hidden = w1.shape[-1]
    kv = lambda bi, i: (i // N_KV_GROUPS, 0, 0)
    return pl.pallas_call(
        _compress_kernel,
        grid=(b, n4),
        in_specs=[
            pl.BlockSpec((1, 1, nb, CMP_STRIDE * HEAD_DIM), lambda bi, i: (bi, i, 0, 0)),
            pl.BlockSpec((1, 1, CMP_BLOCK * HEAD_DIM), kv),
            pl.BlockSpec((1, CMP_BLOCK * HEAD_DIM, hidden), kv),
            pl.BlockSpec((1, 1, hidden), kv),
            pl.BlockSpec((1, hidden, HEAD_DIM), kv),
            pl.BlockSpec((1, 1, HEAD_DIM), kv),
        ],
        out_specs=(
            pl.BlockSpec((1, 1, nb, HEAD_DIM), lambda bi, i: (bi, i, 0, 0)),
            pl.BlockSpec((1, 1, HEAD_DIM, nb), lambda bi, i: (bi, i, 0, 0)),
        ),
        out_shape=(
            jax.ShapeDtypeStruct((b, n4, nb, HEAD_DIM), BF16),
            jax.ShapeDtypeStruct((b, n4, HEAD_DIM, nb), BF16),
        ),
        compiler_params=_params("arbitrary", "arbitrary"),
        name="kv_compress",
    )(a, pos, w1, b1, w2, b2)


def _flash_step(s, v_t, m_old, acc_ref):
    m_new = jnp.maximum(m_old, jnp.max(s, axis=0, keepdims=True))
    alpha = jnp.exp2(m_old - m_new)
    p = jnp.exp2(s - m_new)
    acc_ref[...] = acc_ref[...] * alpha + _dot(v_t, p.astype(BF16))
    return m_new


def _nsa_kernel(qT_ref, kc_ref, vcT_ref, ksl_ref, vslT_ref, kwin_ref, vwinT_ref, gT_ref, y_ref,
                qcat_ref, pbuf_ref, score_ref, sel_ref, wq_ref, sa_ref, sb_ref, ml_ref, wbuf_ref,
                ocmp_ref, aslc_ref, awin_ref):
    n = pl.program_id(2)
    q_start = n * Q_BLOCK
    nq = HEADS_PER_GROUP * Q_BLOCK
    nb = kc_ref.shape[2]
    ns = sel_ref.shape[0]

    for r in range(HEADS_PER_GROUP):
        qcat_ref[:, r * Q_BLOCK:(r + 1) * Q_BLOCK] = qT_ref[0, r * HEAD_DIM:(r + 1) * HEAD_DIM, :]
    qc = qcat_ref[...]
    tin = lax.broadcasted_iota(I32, (1, nq), 1) & (Q_BLOCK - 1)
    tok = q_start + tin

    sel_ref[...] = jnp.full((ns, Q_BLOCK), NEG, F32)
    t1 = q_start + lax.broadcasted_iota(I32, (1, Q_BLOCK), 1)
    tj = t1 // SEL_BLOCK

    def compress_and_select(rows):
        srows = rows // SEL_RATIO
        s = _dot(kc_ref[0, 0, 0:rows, :], qc)
        ci = lax.broadcasted_iota(I32, (rows, nq), 0)
        s = jnp.where(ci * CMP_STRIDE + (CMP_BLOCK - 1) <= tok, s, NEG)
        m = jnp.max(s, axis=0, keepdims=True)
        e = jnp.exp2(s - m)
        l = jnp.sum(e, axis=0, keepdims=True)
        inv = jnp.where(tok >= CMP_BLOCK - 1, 1.0 / l, 0.0)
        p = e * inv
        ocmp_ref[...] = _dot(vcT_ref[0, 0, :, 0:rows], p.astype(BF16))

        psum = p[:, 0:Q_BLOCK]
        for r in range(1, HEADS_PER_GROUP):
            psum = psum + p[:, r * Q_BLOCK:(r + 1) * Q_BLOCK]
        pbuf_ref[0:SUBLANES, :] = jnp.zeros((SUBLANES, Q_BLOCK), F32)
        pbuf_ref[SUBLANES:SUBLANES + rows, :] = psum
        imp = pbuf_ref[pl.ds(SUBLANES - 1, srows, stride=SEL_RATIO), :]
        for d in range(SEL_RATIO):
            imp = imp + pbuf_ref[pl.ds(SUBLANES + d, srows, stride=SEL_RATIO), :]

        ji = lax.broadcasted_iota(I32, (srows, Q_BLOCK), 0)
        forced = (ji == 0) | (ji == tj) | (ji == tj - 1)
        score_ref[0:srows, :] = jnp.where(forced, BIG, jnp.where(ji * SEL_BLOCK <= t1, imp, NEG))

        def pick(_, carry):
            sc = score_ref[0:srows, :]
            mx = jnp.max(sc, axis=0, keepdims=True)
            first = jnp.min(jnp.where(sc == mx, ji, srows), axis=0, keepdims=True)
            hit = ji == first
            sel_ref[0:srows, :] = jnp.where(hit & (mx >= 0.0), 0.0, sel_ref[0:srows, :])
            score_ref[0:srows, :] = jnp.where(hit, BELOW_NEG, sc)
            return carry

        lax.fori_loop(0, min(N_SELECT, srows), pick, 0)

    per_q = Q_BLOCK // CMP_STRIDE
    classes = sorted({r for r in (nb // 4, nb // 2) if r % (SEL_RATIO * SUBLANES * 2) == 0} | {nb})
    for k, rows in enumerate(classes):
        lo = 0 if k == 0 else classes[k - 1] // per_q
        in_class = (n >= lo) if k == len(classes) - 1 else ((n >= lo) & (n < rows // per_q))
        pl.when(in_class)(functools.partial(compress_and_select, rows))

    sel = sel_ref[...]
    for h in range(wq_ref.shape[0]):
        rows_h = min(LANES, ns - LANES * h)
        wq_ref[h, 0:HEAD_DIM, :] = qc
        blk = sel[LANES * h:LANES * h + rows_h].astype(BF16)
        for r in range(HEADS_PER_GROUP):
            wq_ref[h, HEAD_DIM:HEAD_DIM + rows_h, r * Q_BLOCK:(r + 1) * Q_BLOCK] = blk
        if rows_h < LANES:
            wq_ref[h, HEAD_DIM + rows_h:HEAD_DIM + LANES, :] = jnp.zeros((LANES - rows_h, nq), BF16)

    m0 = jnp.full((1, nq), NEG, F32)

    aslc_ref[...] = jnp.zeros(aslc_ref.shape, F32)
    blocks_per_chunk = SLC_CHUNK // SEL_BLOCK

    ml_ref[0:1, :] = m0

    def slc_scores(c, buf_ref):
        k0 = pl.multiple_of(c * SLC_CHUNK, SLC_CHUNK)
        buf_ref[...] = _dot(ksl_ref[0, 0, pl.ds(k0, SLC_CHUNK), :], wq_ref[(c * blocks_per_chunk) // LANES])

    def slc_flash(c, buf_ref, diagonal):
        sc = buf_ref[...]
        if diagonal:
            kpos = c * SLC_CHUNK + lax.broadcasted_iota(I32, (SLC_CHUNK, nq), 0)
            sc = jnp.where(kpos <= tok, sc, NEG)
        ml_ref[0:1, :] = _flash_step(sc, vslT_ref[0, 0, c], ml_ref[0:1, :], aslc_ref)

    def slc_pair(c):
        slc_scores(c + 1, sb_ref)
        slc_flash(c, sa_ref, False)
        slc_scores(c + 2, sa_ref)
        slc_flash(c + 1, sb_ref, False)

    def slc_quad(i, carry):
        slc_pair(4 * i)
        slc_pair(4 * i + 2)
        return carry

    n_past = q_start // SLC_CHUNK
    slc_scores(0, sa_ref)
    lax.fori_loop(0, n_past // 4, slc_quad, 0)
    c_pair = n_past & ~3

    @pl.when((n_past & 2) != 0)
    def _():
        slc_pair(c_pair)

    c_last = n_past & ~1

    @pl.when((n_past & 1) != 0)
    def _():
        slc_scores(c_last + 1, sb_ref)
        slc_flash(c_last, sa_ref, False)
        slc_flash(c_last + 1, sb_ref, True)

    @pl.when((n_past & 1) == 0)
    def _():
        slc_flash(c_last, sa_ref, True)

    n_wc = WINDOW // WIN_CHUNK + 1
    kk = lax.broadcasted_iota(I32, (WIN_CHUNK, nq), 0)
    chunk_ids = []
    m_win = m0
    for i in range(n_wc):
        wb = n - (n_wc - 1) + i
        wbc = jnp.maximum(wb, 0)
        chunk_ids.append(wbc)
        k0 = pl.multiple_of(wbc * WIN_CHUNK, WIN_CHUNK)
        sc = _dot(kwin_ref[0, 0, pl.ds(k0, WIN_CHUNK), :], qc)
        if i == 0:
            sc = jnp.where(kk > tin, sc, NEG)
        if i == n_wc - 1:
            sc = jnp.where(kk <= tin, sc, NEG)
        else:
            sc = sc + jnp.where(wb >= 0, 0.0, NEG)
        wbuf_ref[i] = sc
        m_win = jnp.maximum(m_win, jnp.max(sc, axis=0, keepdims=True))
    for i in range(n_wc):
        p = jnp.exp2(wbuf_ref[i] - m_win)
        pv = _dot(vwinT_ref[0, 0, chunk_ids[i]], p.astype(BF16))
        if i == 0:
            awin_ref[...] = pv
        else:
            awin_ref[...] = awin_ref[...] + pv

    gates = gT_ref[0, 0]
    inv_slc = 1.0 / aslc_ref[HEAD_DIM:HEAD_DIM + 1, :]
    inv_win = 1.0 / awin_ref[HEAD_DIM:HEAD_DIM + 1, :]
    for r in range(HEADS_PER_GROUP):
        cols = slice(r * Q_BLOCK, (r + 1) * Q_BLOCK)
        o = (gates[3 * r:3 * r + 1] * ocmp_ref[:, cols]
             + gates[3 * r + 1:3 * r + 2] * inv_slc[:, cols] * aslc_ref[0:HEAD_DIM, cols]
             + gates[3 * r + 2:3 * r + 3] * inv_win[:, cols] * awin_ref[0:HEAD_DIM, cols])
        y_ref[0, :, r * HEAD_DIM:(r + 1) * HEAD_DIM] = o.T.astype(BF16)


def _nsa(qT, kc, kcT, ksl, vslT, kwin, vwinT, gT):
    b, _, s = qT.shape
    nb = kc.shape[2]
    ns = s // SEL_BLOCK
    g4 = HEADS_PER_GROUP * HEAD_DIM
    nq = HEADS_PER_GROUP * Q_BLOCK
    gates = gT.reshape(b, N_KV_GROUPS, HEADS_PER_GROUP * 3, s)
    per_bg = lambda bi, g, n: (bi, g, 0, 0)
    per_bg5 = lambda bi, g, n: (bi, g, 0, 0, 0)
    once = pl.Buffered(1)
    return pl.pallas_call(
        _nsa_kernel,
        grid=(b, N_KV_GROUPS, s // Q_BLOCK),
        in_specs=[
            pl.BlockSpec((1, g4, Q_BLOCK), lambda bi, g, n: (bi, g, n)),
            pl.BlockSpec((1, 1, nb, HEAD_DIM), per_bg, pipeline_mode=once),
            pl.BlockSpec((1, 1, HEAD_DIM, nb), lambda bi, g, n: (bi, N_KV_GROUPS + g, 0, 0), pipeline_mode=once),
            pl.BlockSpec((1, 1, s, 2 * HEAD_DIM), per_bg, pipeline_mode=once),
            pl.BlockSpec((1, 1, s // SLC_CHUNK, V_ROWS, SLC_CHUNK), per_bg5, pipeline_mode=once),
            pl.BlockSpec((1, 1, s, HEAD_DIM), per_bg, pipeline_mode=once),
            pl.BlockSpec((1, 1, s // WIN_CHUNK, V_ROWS, WIN_CHUNK), per_bg5, pipeline_mode=once),
            pl.BlockSpec((1, 1, HEADS_PER_GROUP * 3, Q_BLOCK), lambda bi, g, n: (bi, g, 0, n)),
        ],
        out_specs=pl.BlockSpec((1, Q_BLOCK, g4), lambda bi, g, n: (bi, n, g)),
        out_shape=jax.ShapeDtypeStruct((b, s, NSA_WIDTH), BF16),
        scratch_shapes=[
            pltpu.VMEM((HEAD_DIM, nq), BF16),
            pltpu.VMEM((nb + SUBLANES, Q_BLOCK), F32),
            pltpu.VMEM((ns, Q_BLOCK), F32),
            pltpu.VMEM((ns, Q_BLOCK), F32),
            pltpu.VMEM((pl.cdiv(ns, LANES), 2 * HEAD_DIM, nq), BF16),
            pltpu.VMEM((SLC_CHUNK, nq), F32),
            pltpu.VMEM((SLC_CHUNK, nq), F32),
            pltpu.VMEM((SUBLANES, nq), F32),
            pltpu.VMEM((WINDOW // WIN_CHUNK + 1, WIN_CHUNK, nq), F32),
            pltpu.VMEM((HEAD_DIM, nq), F32),
            pltpu.VMEM((V_ROWS, nq), F32),
            pltpu.VMEM((V_ROWS, nq), F32),
        ],
        compiler_params=_params("arbitrary", "arbitrary", "arbitrary"),
        name="nsa_attention",
    )(qT, kc, kcT, ksl, vslT, kwin, vwinT, gates)


def _outproj_kernel(yp_ref, yn_ref, wo_ref, x_ref, g1_ref, n2_ref, sc_ref, sh_ref, wrT_ref, br_ref, tri_ref,
                    x1_ref, h2_ref, idx_ref, wt_ref, rank_ref, cnt_ref, carry_ref):
    tm = x_ref.shape[1]
    n_exp = wrT_ref.shape[0]

    @pl.when((pl.program_id(0) == 0) & (pl.program_id(1) == 0))
    def _():
        carry_ref[...] = jnp.zeros(carry_ref.shape, F32)

    mix = _dot(yp_ref[0], wo_ref[0:POOL_WIDTH, :]) + _dot(yn_ref[0], wo_ref[POOL_WIDTH:POOL_WIDTH + NSA_WIDTH, :])
    x1 = x_ref[0] + g1_ref[0] * mix
    x1_ref[0] = x1
    h2 = _rmsnorm(x1, n2_ref[...]) * (1.0 + sc_ref[0]) + sh_ref[0]
    h2_ref[0] = h2

    logits = lax.dot_general(wrT_ref[...], h2.astype(BF16), (((1,), (1,)), ((), ())),
                             preferred_element_type=F32) + br_ref[...]
    ei = lax.broadcasted_iota(I32, (n_exp, tm), 0)
    vals, ids = [], []
    for _ in range(TOP_K):
        mx = jnp.max(logits, axis=0, keepdims=True)
        first = jnp.min(jnp.where(logits == mx, ei, n_exp), axis=0, keepdims=True)
        vals.append(mx)
        ids.append(first)
        logits = jnp.where(ei == first, -jnp.inf, logits)
    exps = [jnp.exp(v - vals[0]) for v in vals]
    den = exps[0]
    for e in exps[1:]:
        den = den + e

    onehot = jnp.zeros((n_exp, tm), F32)
    for first in ids:
        onehot = onehot + (ei == first).astype(F32)
    before = _dot(onehot.astype(BF16), tri_ref[...]) + carry_ref[:, 0:1]
    wt_ref[TOP_K:SUBLANES, :] = jnp.zeros((SUBLANES - TOP_K, tm), F32)
    for k in range(TOP_K):
        idx_ref[k:k + 1, :] = ids[k]
        wt_ref[k:k + 1, :] = exps[k] / den
        rank = jnp.sum(jnp.where(ei == ids[k], before, 0.0), axis=0, keepdims=True)
        rank_ref[k:k + 1, :] = rank.astype(I32)
    carry_ref[...] = carry_ref[...] + jnp.sum(onehot, axis=1, keepdims=True)
    cnt_ref[...] = carry_ref[...].astype(I32)


def _output_projection(y_pool, y_nsa, w_out, x, gate1, norm_g, scale, shift, w_router_t, b_router, tri):
    b, s, d = x.shape
    tm = PROJ_TM
    n_exp = w_router_t.shape[0]
    t_all = b * s
    tok = lambda bi, m: (bi, m, 0)
    per_b = lambda bi, m: (bi, 0, 0)
    const2 = lambda bi, m: (0, 0)
    flat = lambda bi, m: (0, bi * (s // tm) + m)
    return pl.pallas_call(
        _outproj_kernel,
        grid=(b, s // tm),
        in_specs=[
            pl.BlockSpec((1, tm, POOL_WIDTH), tok),
            pl.BlockSpec((1, tm, NSA_WIDTH), tok),
            pl.BlockSpec((POOL_WIDTH + NSA_WIDTH, d), const2, pipeline_mode=pl.Buffered(1)),
            pl.BlockSpec((1, tm, d), tok),
            pl.BlockSpec((1, 1, d), per_b),
            pl.BlockSpec((1, d), const2),
            pl.BlockSpec((1, 1, d), per_b),
            pl.BlockSpec((1, 1, d), per_b),
            pl.BlockSpec((n_exp, d), const2),
            pl.BlockSpec((n_exp, 1), const2),
            pl.BlockSpec((tm, tm), const2, pipeline_mode=pl.Buffered(1)),
        ],
        out_specs=(
            pl.BlockSpec((1, tm, d), tok),
            pl.BlockSpec((1, tm, d), tok),
            pl.BlockSpec((TOP_K, tm), flat),
            pl.BlockSpec((SUBLANES, tm), flat),
            pl.BlockSpec((TOP_K, tm), flat),
            pl.BlockSpec((n_exp, LANES), const2),
        ),
        out_shape=(
            jax.ShapeDtypeStruct((b, s, d), F32),
            jax.ShapeDtypeStruct((b, s, d), F32),
            jax.ShapeDtypeStruct((TOP_K, t_all), I32),
            jax.ShapeDtypeStruct((SUBLANES, t_all), F32),
            jax.ShapeDtypeStruct((TOP_K, t_all), I32),
            jax.ShapeDtypeStruct((n_exp, LANES), I32),
        ),
        scratch_shapes=[pltpu.VMEM((n_exp, LANES), F32)],
        compiler_params=_params("arbitrary", "arbitrary"),
        name="output_projection_router",
    )(y_pool, y_nsa, w_out, x, gate1, norm_g, scale, shift, w_router_t, b_router, tri)


def _dest_kernel(ps_ref, idx_ref, rank_ref, dest_ref):
    idx = idx_ref[...]
    dest = rank_ref[...]
    for e in range(ps_ref.shape[0]):
        dest = dest + jnp.where(idx == e, ps_ref[e], 0)
    dest_ref[...] = dest


def _destinations(seg_starts, idx_t, rank_t):
    return pl.pallas_call(
        _dest_kernel,
        in_specs=[
            pl.BlockSpec(memory_space=pltpu.SMEM),
            pl.BlockSpec(memory_space=pltpu.VMEM),
            pl.BlockSpec(memory_space=pltpu.VMEM),
        ],
        out_specs=pl.BlockSpec(memory_space=pltpu.VMEM),
        out_shape=jax.ShapeDtypeStruct(idx_t.shape, I32),
        name="moe_destinations",
    )(seg_starts, idx_t, rank_t)


def _row_copy(src_ref, src_row, dst_ref, dst_row, sem):
    return pltpu.make_async_copy(src_ref.at[pl.ds(src_row, 1)], dst_ref.at[pl.ds(dst_row, 1)], sem)


def _dispatch_kernel(dest_ref, starts_ref, ends_ref, h_ref, xs_ref, zero_ref, sem, zsem):
    tm = h_ref.shape[0]
    n_exp = starts_ref.shape[0]
    zrows = zero_ref.shape[0]

    @pl.when(pl.program_id(0) == 0)
    def _():
        zero_ref[...] = jnp.zeros(zero_ref.shape, zero_ref.dtype)

        def clear_block(end_row, start):
            for r in range(EXPERT_TM // zrows):
                row0 = pl.multiple_of(end_row - EXPERT_TM + r * zrows, zrows)
                cp = pltpu.make_async_copy(zero_ref, xs_ref.at[pl.ds(row0, zrows)], zsem)
                cp.start() if start else cp.wait()

        def clear(e, start):
            @pl.when(ends_ref[e] > starts_ref[e])
            def _():
                clear_block(ends_ref[e], start)

        first_unused = ends_ref[n_exp - 1] // EXPERT_TM
        n_blk = xs_ref.shape[0] // EXPERT_TM
        for start in (True, False):
            lax.fori_loop(0, n_exp, lambda e, c: (clear(e, start), c)[1], 0)
            lax.fori_loop(first_unused, n_blk, lambda i, c: (clear_block((i + 1) * EXPERT_TM, start), c)[1], 0)

    def issue(t, carry):
        for k in range(TOP_K):
            _row_copy(h_ref, t, xs_ref, dest_ref[k, t], sem).start()
        return carry

    lax.fori_loop(0, tm, issue, 0)

    def drain(t, carry):
        for k in range(TOP_K):
            _row_copy(h_ref, t, xs_ref, dest_ref[k, t], sem).wait()
        return carry

    lax.fori_loop(0, tm, drain, 0)


def _dispatch(dest_t, seg_starts, seg_ends, h_rows, n_buf):
    t_all, d = h_rows.shape
    tm = DISPATCH_TM
    return pl.pallas_call(
        _dispatch_kernel,
        grid=(t_all // tm,),
        in_specs=[
            pl.BlockSpec((TOP_K, tm), lambda i: (0, i), memory_space=pltpu.SMEM),
            pl.BlockSpec(memory_space=pltpu.SMEM),
            pl.BlockSpec(memory_space=pltpu.SMEM),
            pl.BlockSpec((tm, d), lambda i: (i, 0)),
        ],
        out_specs=pl.BlockSpec(memory_space=pl.ANY),
        out_shape=jax.ShapeDtypeStruct((n_buf, d), h_rows.dtype),
        scratch_shapes=[pltpu.VMEM((DISPATCH_ZERO_ROWS, d), h_rows.dtype),
                        pltpu.SemaphoreType.DMA(()), pltpu.SemaphoreType.DMA(())],
        compiler_params=_params("arbitrary"),
        name="moe_dispatch",
    )(dest_t, seg_starts, seg_ends, h_rows)


def _ffn_up_kernel(be_ref, nu_ref, xs_ref, wg_ref, bg_ref, wu_ref, bu_ref, h_ref, xb_ref):
    del be_ref
    i = pl.program_id(0)
    j = pl.program_id(1)
    used = i < nu_ref[0]

    @pl.when(used & (j == 0))
    def _():
        xb_ref[...] = xs_ref[...].astype(BF16)

    @pl.when(used)
    def _():
        xb = xb_ref[...]
        gl = jnp.minimum(_dot(xb, wg_ref[0, 0].astype(BF16)) + bg_ref[0, 0], SWIGLU_LIMIT)
        lin = jnp.clip(_dot(xb, wu_ref[0, 0].astype(BF16)) + bu_ref[0, 0], -SWIGLU_LIMIT, SWIGLU_LIMIT)
        h_ref[...] = (gl * jax.nn.sigmoid(SWIGLU_ALPHA * gl) * (lin + 1.0)).astype(BF16)

    @pl.when(jnp.logical_not(used))
    def _():
        h_ref[...] = jnp.zeros(h_ref.shape, BF16)


def _ffn_down_kernel(be_ref, nu_ref, h_ref, wd_ref, bd_ref, y_ref):
    del be_ref
    used = pl.program_id(0) < nu_ref[0]

    @pl.when(used)
    def _():
        y_ref[...] = _dot(h_ref[...], wd_ref[0, 0].astype(BF16)) + bd_ref[0, 0]

    @pl.when(jnp.logical_not(used))
    def _():
        y_ref[...] = jnp.zeros(y_ref.shape, F32)


def _expert_ffn(layer, block_expert, n_used, xs, w_gate, b_gate, w_up, b_up, w_down, b_down):
    n_buf, d = xs.shape
    depth, n_exp, _, f = w_gate.shape
    tm, tf = EXPERT_TM, EXPERT_TF
    nf = f // tf
    nd = d // tf
    n_blk = n_buf // tm

    def rows(i, j, be, nu):
        return (jnp.minimum(i, nu[0] - 1), 0)

    def cols(nj):
        return lambda i, j, be, nu: (layer, be[i], 0, jnp.where(i < nu[0], j, nj - 1))

    hidden = pl.pallas_call(
        _ffn_up_kernel,
        grid_spec=pltpu.PrefetchScalarGridSpec(
            num_scalar_prefetch=2,
            grid=(n_blk, nf),
            in_specs=[
                pl.BlockSpec((tm, d), rows),
                pl.BlockSpec((1, 1, d, tf), cols(nf)),
                pl.BlockSpec((1, 1, 1, tf), cols(nf)),
                pl.BlockSpec((1, 1, d, tf), cols(nf)),
                pl.BlockSpec((1, 1, 1, tf), cols(nf)),
            ],
            out_specs=pl.BlockSpec((tm, tf), lambda i, j, be, nu: (i, j)),
            scratch_shapes=[pltpu.VMEM((tm, d), BF16)],
        ),
        out_shape=jax.ShapeDtypeStruct((n_buf, f), BF16),
        compiler_params=_params("arbitrary", "arbitrary"),
        name="moe_ffn_up",
    )(block_expert, n_used, xs, w_gate, b_gate.reshape(depth, n_exp, 1, f), w_up, b_up.reshape(depth, n_exp, 1, f))
    return pl.pallas_call(
        _ffn_down_kernel,
        grid_spec=pltpu.PrefetchScalarGridSpec(
            num_scalar_prefetch=2,
            grid=(n_blk, nd),
            in_specs=[
                pl.BlockSpec((tm, f), rows),
                pl.BlockSpec((1, 1, f, tf), cols(nd)),
                pl.BlockSpec((1, 1, 1, tf), cols(nd)),
            ],
            out_specs=pl.BlockSpec((tm, tf), lambda i, j, be, nu: (i, j)),
        ),
        out_shape=jax.ShapeDtypeStruct((n_buf, d), F32),
        compiler_params=_params("arbitrary", "arbitrary"),
        name="moe_ffn_down",
    )(block_expert, n_used, hidden, w_down, b_down.reshape(depth, n_exp, 1, d))


def _combine_kernel(dest_ref, wt_ref, x1_ref, g2_ref, fg_ref, y_ref, o_ref, rows_ref, sem, *, final_norm):
    tm = x1_ref.shape[1]

    def issue(t, carry):
        for k in range(TOP_K):
            _row_copy(y_ref, dest_ref[k, t], rows_ref.at[k], t, sem).start()
        return carry

    lax.fori_loop(0, tm, issue, 0)

    wt = jnp.concatenate([wt_ref[...], jnp.zeros((tm - SUBLANES, tm), F32)], axis=0).T

    def drain(t, carry):
        for k in range(TOP_K):
            _row_copy(y_ref, dest_ref[k, t], rows_ref.at[k], t, sem).wait()
        return carry

    lax.fori_loop(0, tm, drain, 0)

    acc = wt[:, 0:1] * rows_ref[0]
    for k in range(1, TOP_K):
        acc = acc + wt[:, k:k + 1] * rows_ref[k]
    out = x1_ref[0] + g2_ref[0] * acc
    if final_norm:
        out = _rmsnorm(out, fg_ref[...])
    o_ref[0] = out


def _combine(dest_t, wt_t, x1, gate2, final_g, y_rows, final_norm):
    b, s, d = x1.shape
    tm = COMBINE_TM
    flat = lambda bi, m: (0, bi * (s // tm) + m)
    return pl.pallas_call(
        functools.partial(_combine_kernel, final_norm=final_norm),
        grid=(b, s // tm),
        in_specs=[
            pl.BlockSpec((TOP_K, tm), flat, memory_space=pltpu.SMEM),
            pl.BlockSpec((SUBLANES, tm), flat),
            pl.BlockSpec((1, tm, d), lambda bi, m: (bi, m, 0)),
            pl.BlockSpec((1, 1, d), lambda bi, m: (bi, 0, 0)),
            pl.BlockSpec((1, d), lambda bi, m: (0, 0)),
            pl.BlockSpec(memory_space=pl.ANY),
        ],
        out_specs=pl.BlockSpec((1, tm, d), lambda bi, m: (bi, m, 0)),
        out_shape=jax.ShapeDtypeStruct((b, s, d), F32),
        scratch_shapes=[pltpu.VMEM((TOP_K, tm, d), F32), pltpu.SemaphoreType.DMA(())],
        compiler_params=_params("arbitrary", "arbitrary"),
        name="moe_combine",
    )(dest_t, wt_t, x1, gate2, final_g, y_rows)


def _segment_layout(counts, n_blk):
    padded = (counts + EXPERT_TM - 1) // EXPERT_TM * EXPERT_TM
    ends = jnp.cumsum(padded)
    starts = ends - padded
    n_used = ends[-1] // EXPERT_TM
    blk = jnp.minimum(jnp.arange(n_blk, dtype=I32), n_used - 1) * EXPERT_TM
    block_expert = jnp.minimum(jnp.searchsorted(ends, blk, side="right"), counts.shape[0] - 1)
    return starts.astype(I32), ends.astype(I32), block_expert.astype(I32), n_used.reshape(1).astype(I32)


def kernel(x, c, norm1_g, norm2_g, w_ada, b_ada, w_in, pool_w, pool_scale, cmp_pos_k, cmp_w1_k, cmp_b1_k, cmp_w2_k, cmp_b2_k, cmp_pos_v, cmp_w1_v, cmp_b1_v, cmp_w2_v, cmp_b2_v, w_out, w_router, b_router, w_gate, b_gate, w_up, b_up, w_down, b_down, final_g):
    b, s, d = x.shape
    depth = w_ada.shape[0]
    n_exp = w_router.shape[-1]
    t_all = b * s
    assert s % PROJ_TM == 0 and s % SLC_CHUNK == 0 and t_all % DISPATCH_TM == 0
    n_main = POOL_WIDTH + NSA_WIDTH + 6 * KV_WIDTH
    assert w_in.shape[-1] == n_main + N_GATE_COLS

    c_pad = jnp.pad(c, ((0, SUBLANES - b), (0, 0)))
    mod = _modulation(c_pad, w_ada, b_ada)[:, :b]
    tri = (jnp.arange(PROJ_TM)[:, None] < jnp.arange(PROJ_TM)[None, :]).astype(BF16)
    n_buf = (t_all * TOP_K + n_exp * (EXPERT_TM - 1)) // EXPERT_TM * EXPERT_TM
    n_blk = n_buf // EXPERT_TM

    for l in range(depth):
        sh1, sc1, g1, sh2, sc2, g2 = [m.reshape(b, 1, d) for m in jnp.split(mod[l], 6, axis=-1)]
        w_main = w_in[l, :, :n_main].astype(BF16)
        w_gates = jnp.pad(w_in[l, :, n_main:], ((0, 0), (0, LANES - N_GATE_COLS))).astype(BF16)
        u_pool, qT, kvc, ksl, vslT, kwin, vwinT, gT = _input_projection(
            x, norm1_g[l].reshape(1, d), sc1, sh1, w_main, w_gates)
        y_pool = _pool_mixer(u_pool, pool_w[l].astype(BF16), pool_scale[l].reshape(1, POOL_WIDTH))
        kc, kcT = _compress(
            kvc,
            jnp.stack([cmp_pos_k[l], cmp_pos_v[l]]).reshape(2, 1, CMP_BLOCK * HEAD_DIM),
            jnp.stack([cmp_w1_k[l], cmp_w1_v[l]]).astype(BF16),
            jnp.stack([cmp_b1_k[l], cmp_b1_v[l]])[:, None, :],
            jnp.stack([cmp_w2_k[l], cmp_w2_v[l]]).astype(BF16),
            jnp.stack([cmp_b2_k[l], cmp_b2_v[l]])[:, None, :])
        y_nsa = _nsa(qT, kc, kcT, ksl, vslT, kwin, vwinT, gT)
        x1, h2, idx_t, wt_t, rank_t, counts = _output_projection(
            y_pool, y_nsa, w_out[l].astype(BF16), x, g1, norm2_g[l].reshape(1, d), sc2, sh2,
            w_router[l].T.astype(BF16), b_router[l].reshape(n_exp, 1), tri)
        seg_starts, seg_ends, block_expert, n_used = _segment_layout(counts[:, 0], n_blk)
        dest_t = _destinations(seg_starts, idx_t, rank_t)
        xs = _dispatch(dest_t, seg_starts, seg_ends, h2.reshape(t_all, d), n_buf)
        y_rows = _expert_ffn(l, block_expert, n_used, xs, w_gate, b_gate, w_up, b_up, w_down, b_down)
        x = _combine(dest_t, wt_t, x1, g2, final_g.reshape(1, d), y_rows, final_norm=(l == depth - 1))
    return x
```

```python
import functools

import jax
import jax.numpy as jnp
from jax import lax
from jax.experimental import pallas as pl
from jax.experimental.pallas import tpu as pltpu

F32 = jnp.float32
BF16 = jnp.bfloat16
I32 = jnp.int32

HEAD_DIM = 128
N_HEADS = 8
N_KV_GROUPS = 2
HEADS_PER_GROUP = N_HEADS // N_KV_GROUPS
POOL_WINDOWS = (2, 4, 8, 16)
POOL_GROUP = 256
POOL_WIDTH = POOL_GROUP * len(POOL_WINDOWS)
NSA_WIDTH = N_HEADS * HEAD_DIM
KV_WIDTH = N_KV_GROUPS * HEAD_DIM
N_GATE_COLS = 3 * N_HEADS
CMP_BLOCK = 32
CMP_STRIDE = 16
SEL_BLOCK = 64
SEL_RATIO = SEL_BLOCK // CMP_STRIDE
N_SELECT = 16
WINDOW = 512
Q_BLOCK = 128
TOP_K = 4
SWIGLU_ALPHA = 1.702
SWIGLU_LIMIT = 7.0
NORM_EPS = 1e-5
NEG = -1e30
BIG = 1e30
BELOW_NEG = -3e38
V_ROWS = HEAD_DIM + 16
LOG2_E = 1.4426950408889634

LANES = 128
SUBLANES = 8
VMEM_LIMIT_BYTES = 56 * 1024 * 1024

MOD_TN = 1024
PROJ_TM = 512
SLC_CHUNK = 256
SLC_UNROLL = 6
WIN_CHUNK = 128
DISPATCH_TM = 256
DISPATCH_ZERO_ROWS = 256
COMBINE_TM = 128
EXPERT_TM = 1024
EXPERT_TF = 512


def _params(*sem):
    return pltpu.CompilerParams(dimension_semantics=sem, vmem_limit_bytes=VMEM_LIMIT_BYTES)


def _dot(a, b):
    return jnp.dot(a, b, preferred_element_type=F32)


def _rmsnorm(x, g):
    return x * lax.rsqrt(jnp.mean(x * x, axis=-1, keepdims=True) + NORM_EPS) * g


def _mod_kernel(c_ref, w_ref, b_ref, o_ref):
    c = c_ref[...]
    ca = (c * jax.nn.sigmoid(c)).astype(BF16)
    o_ref[0] = _dot(ca, w_ref[0].astype(BF16)) + b_ref[0]


def _modulation(c_pad, w_ada, b_ada):
    depth, d, n = w_ada.shape
    rows = c_pad.shape[0]
    return pl.pallas_call(
        _mod_kernel,
        grid=(depth, n // MOD_TN),
        in_specs=[
            pl.BlockSpec((rows, d), lambda l, j: (0, 0)),
            pl.BlockSpec((1, d, MOD_TN), lambda l, j: (l, 0, j)),
            pl.BlockSpec((1, 1, MOD_TN), lambda l, j: (l, 0, j)),
        ],
        out_specs=pl.BlockSpec((1, rows, MOD_TN), lambda l, j: (l, 0, j)),
        out_shape=jax.ShapeDtypeStruct((depth, rows, n), F32),
        compiler_params=_params("arbitrary", "arbitrary"),
        name="adaln_modulation",
    )(c_pad, w_ada, b_ada.reshape(depth, 1, n))


def _inproj_kernel(x_ref, g_ref, sc_ref, sh_ref, w_ref, wg_ref,
                   upool_ref, qT_ref, kvc_ref, ksl_ref, vslT_ref, kwin_ref, vwinT_ref, gT_ref):
    tm = x_ref.shape[1]
    h = _rmsnorm(x_ref[0], g_ref[...]) * (1.0 + sc_ref[0]) + sh_ref[0]
    hb = h.astype(BF16)
    upool_ref[0] = _dot(hb, w_ref[:, 0:POOL_WIDTH])
    q0 = POOL_WIDTH
    scale = HEAD_DIM ** -0.5 * LOG2_E
    def four_heads(col0):
        wide = _dot(hb, w_ref[:, col0:col0 + 4 * HEAD_DIM])
        return [wide[:, i * HEAD_DIM:(i + 1) * HEAD_DIM] for i in range(4)]

    for j in range(N_HEADS // 4):
        for i, q in enumerate(four_heads(q0 + j * 4 * HEAD_DIM)):
            hd = 4 * j + i
            qT_ref[0, hd * HEAD_DIM:(hd + 1) * HEAD_DIM, :] = (q * scale).T.astype(BF16)
    c0 = q0 + NSA_WIDTH
    for i, kv in enumerate(four_heads(c0)):
        kvc_ref[0, i] = kv
    s0 = c0 + 2 * KV_WIDTH
    w0 = s0 + 2 * KV_WIDTH
    slc = four_heads(s0)
    win = four_heads(w0)
    tpos = pl.program_id(1) * tm + lax.broadcasted_iota(I32, (tm, HEAD_DIM), 0)
    lane = lax.broadcasted_iota(I32, (tm, HEAD_DIM), 1)
    block_onehot = jnp.where(lane == ((tpos // SEL_BLOCK) & (LANES - 1)), 1.0, 0.0).astype(BF16)
    for g in range(N_KV_GROUPS):
        ksl_ref[0, g, :, 0:HEAD_DIM] = slc[g].astype(BF16)
        ksl_ref[0, g, :, HEAD_DIM:2 * HEAD_DIM] = block_onehot
        vT = slc[N_KV_GROUPS + g].T.astype(BF16)
        for c in range(tm // SLC_CHUNK):
            vslT_ref[0, g, c, 0:HEAD_DIM, :] = vT[:, c * SLC_CHUNK:(c + 1) * SLC_CHUNK]
            vslT_ref[0, g, c, HEAD_DIM:V_ROWS, :] = jnp.ones((V_ROWS - HEAD_DIM, SLC_CHUNK), BF16)
        kwin_ref[0, g] = win[g].astype(BF16)
        vT = win[N_KV_GROUPS + g].T.astype(BF16)
        for c in range(tm // WIN_CHUNK):
            vwinT_ref[0, g, c, 0:HEAD_DIM, :] = vT[:, c * WIN_CHUNK:(c + 1) * WIN_CHUNK]
            vwinT_ref[0, g, c, HEAD_DIM:V_ROWS, :] = jnp.ones((V_ROWS - HEAD_DIM, WIN_CHUNK), BF16)
    gl = jax.nn.sigmoid(_dot(hb, wg_ref[...]))
    gT_ref[0] = gl.T[0:N_GATE_COLS, :]


def _input_projection(x, norm_g, scale, shift, w_main, w_gates):
    b, s, d = x.shape
    tm = PROJ_TM
    n_main = w_main.shape[1]
    tok = lambda bi, m: (bi, m, 0)
    per_b = lambda bi, m: (bi, 0, 0)
    const2 = lambda bi, m: (0, 0)
    out_shape = (
        jax.ShapeDtypeStruct((b, s, POOL_WIDTH), F32),
        jax.ShapeDtypeStruct((b, NSA_WIDTH, s), BF16),
        jax.ShapeDtypeStruct((b, 2 * N_KV_GROUPS, s, HEAD_DIM), F32),
        jax.ShapeDtypeStruct((b, N_KV_GROUPS, s, 2 * HEAD_DIM), BF16),
        jax.ShapeDtypeStruct((b, N_KV_GROUPS, s // SLC_CHUNK, V_ROWS, SLC_CHUNK), BF16),
        jax.ShapeDtypeStruct((b, N_KV_GROUPS, s, HEAD_DIM), BF16),
        jax.ShapeDtypeStruct((b, N_KV_GROUPS, s // WIN_CHUNK, V_ROWS, WIN_CHUNK), BF16),
        jax.ShapeDtypeStruct((b, N_GATE_COLS, s), F32),
    )
    out_specs = (
        pl.BlockSpec((1, tm, POOL_WIDTH), tok),
        pl.BlockSpec((1, NSA_WIDTH, tm), lambda bi, m: (bi, 0, m)),
        pl.BlockSpec((1, 2 * N_KV_GROUPS, tm, HEAD_DIM), lambda bi, m: (bi, 0, m, 0)),
        pl.BlockSpec((1, N_KV_GROUPS, tm, 2 * HEAD_DIM), lambda bi, m: (bi, 0, m, 0)),
        pl.BlockSpec((1, N_KV_GROUPS, tm // SLC_CHUNK, V_ROWS, SLC_CHUNK), lambda bi, m: (bi, 0, m, 0, 0)),
        pl.BlockSpec((1, N_KV_GROUPS, tm, HEAD_DIM), lambda bi, m: (bi, 0, m, 0)),
        pl.BlockSpec((1, N_KV_GROUPS, tm // WIN_CHUNK, V_ROWS, WIN_CHUNK), lambda bi, m: (bi, 0, m, 0, 0)),
        pl.BlockSpec((1, N_GATE_COLS, tm), lambda bi, m: (bi, 0, m)),
    )
    return pl.pallas_call(
        _inproj_kernel,
        grid=(b, s // tm),
        in_specs=[
            pl.BlockSpec((1, tm, d), tok),
            pl.BlockSpec((1, d), const2),
            pl.BlockSpec((1, 1, d), per_b),
            pl.BlockSpec((1, 1, d), per_b),
            pl.BlockSpec((d, n_main), const2, pipeline_mode=pl.Buffered(1)),
            pl.BlockSpec((d, LANES), const2, pipeline_mode=pl.Buffered(1)),
        ],
        out_specs=out_specs,
        out_shape=out_shape,
        compiler_params=_params("arbitrary", "arbitrary"),
        name="input_projection",
    )(x, norm_g, scale, shift, w_main, w_gates)


def _pool_kernel(u_ref, pw_ref, ps_ref, o_ref, ext_ref):
    tm = u_ref.shape[1]
    halo = max(POOL_WINDOWS)
    m = pl.program_id(1)

    @pl.when(m == 0)
    def _():
        ext_ref[0:halo, :] = jnp.zeros((halo, POOL_WIDTH), F32)

    ext_ref[halo:halo + tm, :] = u_ref[0]
    t = m * tm + lax.broadcasted_iota(I32, (tm, 1), 0)
    for gi, w in enumerate(POOL_WINDOWS):
        cols = slice(gi * POOL_GROUP, (gi + 1) * POOL_GROUP)
        ug = ext_ref[halo:halo + tm, cols]
        acc = ug
        for i in range(1, w):
            acc = acc + ext_ref[halo - i:halo - i + tm, cols]
        cnt = jnp.minimum(t + 1, w).astype(F32)
        diff = (acc / cnt - ug).astype(BF16)
        o_ref[0, :, cols] = (_dot(diff, pw_ref[gi]) * ps_ref[:, cols]).astype(BF16)
    ext_ref[0:halo, :] = ext_ref[tm:tm + halo, :]


def _pool_mixer(u_pool, pool_w, pool_scale):
    b, s, _ = u_pool.shape
    tm = PROJ_TM
    return pl.pallas_call(
        _pool_kernel,
        grid=(b, s // tm),
        in_specs=[
            pl.BlockSpec((1, tm, POOL_WIDTH), lambda bi, m: (bi, m, 0)),
            pl.BlockSpec((len(POOL_WINDOWS), POOL_GROUP, POOL_GROUP), lambda bi, m: (0, 0, 0)),
            pl.BlockSpec((1, POOL_WIDTH), lambda bi, m: (0, 0)),
        ],
        out_specs=pl.BlockSpec((1, tm, POOL_WIDTH), lambda bi, m: (bi, m, 0)),
        out_shape=jax.ShapeDtypeStruct((b, s, POOL_WIDTH), BF16),
        scratch_shapes=[pltpu.VMEM((tm + max(POOL_WINDOWS), POOL_WIDTH), F32)],
        compiler_params=_params("arbitrary", "arbitrary"),
        name="pool_mixer",
    )(u_pool, pool_w, pool_scale)


def _compress_kernel(a_ref, pos_ref, w1_ref, b1_ref, w2_ref, b2_ref, o_ref, oT_ref):
    half = CMP_STRIDE * HEAD_DIM
    a = a_ref[0, 0]
    nb = a.shape[0]
    pos = pos_ref[0]
    a_lo = (a + pos[:, 0:half]).astype(BF16)
    a_hi = (a + pos[:, half:2 * half]).astype(BF16)
    p_lo = _dot(a_lo, w1_ref[0, 0:half, :])
    p_hi = _dot(a_hi, w1_ref[0, half:2 * half, :])
    hid = jax.nn.gelu(p_lo + pltpu.roll(p_hi, nb - 1, 0) + b1_ref[0])
    out = _dot(hid.astype(BF16), w2_ref[0]) + b2_ref[0]
    o_ref[0, 0] = out.astype(BF16)
    oT_ref[0, 0] = out.T.astype(BF16)


def _compress(kvc, pos, w1, b1, w2, b2):
    b, n4, s, _ = kvc.shape
    nb = s // CMP_STRIDE
    a = kvc.reshape(b, n4, nb, CMP_STRIDE * HEAD_DIM)
    hidden = w1.shape[-1]
    kv = lambda bi, i: (i // N_KV_GROUPS, 0, 0)
    return pl.pallas_call(
        _compress_kernel,
        grid=(b, n4),
        in_specs=[
            pl.BlockSpec((1, 1, nb, CMP_STRIDE * HEAD_DIM), lambda bi, i: (bi, i, 0, 0)),
            pl.BlockSpec((1, 1, CMP_BLOCK * HEAD_DIM), kv),
            pl.BlockSpec((1, CMP_BLOCK * HEAD_DIM, hidden), kv),
            pl.BlockSpec((1, 1, hidden), kv),
            pl.BlockSpec((1, hidden, HEAD_DIM), kv),
            pl.BlockSpec((1, 1, HEAD_DIM), kv),
        ],
        out_specs=(
            pl.BlockSpec((1, 1, nb, HEAD_DIM), lambda bi, i: (bi, i, 0, 0)),
            pl.BlockSpec((1, 1, HEAD_DIM, nb), lambda bi, i: (bi, i, 0, 0)),
        ),
        out_shape=(
            jax.ShapeDtypeStruct((b, n4, nb, HEAD_DIM), BF16),
            jax.ShapeDtypeStruct((b, n4, HEAD_DIM, nb), BF16),
        ),
        compiler_params=_params("arbitrary", "arbitrary"),
        name="kv_compress",
    )(a, pos, w1, b1, w2, b2)


def _flash_step(s, v_t, m_old, acc_ref):
    m_new = jnp.maximum(m_old, jnp.max(s, axis=0, keepdims=True))
    alpha = jnp.exp2(m_old - m_new)
    p = jnp.exp2(s - m_new)
    acc_ref[...] = acc_ref[...] * alpha + _dot(v_t, p.astype(BF16))
    return m_new


def _nsa_kernel(qT_ref, kc_ref, vcT_ref, ksl_ref, vslT_ref, kwin_ref, vwinT_ref, gT_ref, y_ref,
                qcat_ref, pbuf_ref, score_ref, sel_ref, wq_ref, sa_ref, sb_ref, sc_ref, ml_ref, wbuf_ref,
                ocmp_ref, aslc_ref, awin_ref):
    n = pl.program_id(2)
    q_start = n * Q_BLOCK
    nq = HEADS_PER_GROUP * Q_BLOCK
    nb = kc_ref.shape[2]
    ns = sel_ref.shape[0]

    for r in range(HEADS_PER_GROUP):
        qcat_ref[:, r * Q_BLOCK:(r + 1) * Q_BLOCK] = qT_ref[0, r * HEAD_DIM:(r + 1) * HEAD_DIM, :]
    qc = qcat_ref[...]
    tin = lax.broadcasted_iota(I32, (1, nq), 1) & (Q_BLOCK - 1)
    tok = q_start + tin

    sel_ref[...] = jnp.full((ns, Q_BLOCK), NEG, F32)
    t1 = q_start + lax.broadcasted_iota(I32, (1, Q_BLOCK), 1)
    tj = t1 // SEL_BLOCK

    def compress_and_select(rows):
        srows = rows // SEL_RATIO
        s = _dot(kc_ref[0, 0, 0:rows, :], qc)
        ci = lax.broadcasted_iota(I32, (rows, nq), 0)
        s = jnp.where(ci * CMP_STRIDE + (CMP_BLOCK - 1) <= tok, s, NEG)
        m = jnp.max(s, axis=0, keepdims=True)
        e = jnp.exp2(s - m)
        l = jnp.sum(e, axis=0, keepdims=True)
        inv = jnp.where(tok >= CMP_BLOCK - 1, 1.0 / l, 0.0)
        p = e * inv
        ocmp_ref[...] = _dot(vcT_ref[0, 0, :, 0:rows], p.astype(BF16))

        psum = p[:, 0:Q_BLOCK]
        for r in range(1, HEADS_PER_GROUP):
            psum = psum + p[:, r * Q_BLOCK:(r + 1) * Q_BLOCK]
        pbuf_ref[0:SUBLANES, :] = jnp.zeros((SUBLANES, Q_BLOCK), F32)
        pbuf_ref[SUBLANES:SUBLANES + rows, :] = psum
        imp = pbuf_ref[pl.ds(SUBLANES - 1, srows, stride=SEL_RATIO), :]
        for d in range(SEL_RATIO):
            imp = imp + pbuf_ref[pl.ds(SUBLANES + d, srows, stride=SEL_RATIO), :]

        ji = lax.broadcasted_iota(I32, (srows, Q_BLOCK), 0)
        forced = (ji == 0) | (ji == tj) | (ji == tj - 1)
        sel_ref[0:srows, :] = jnp.where(forced, 0.0, NEG)
        score_ref[0:srows, :] = jnp.where(forced, BELOW_NEG, jnp.where(ji * SEL_BLOCK <= t1, imp, NEG))
        n_forced = 1 + (tj > 0).astype(I32) + (tj > 1).astype(I32)
        quota = min(N_SELECT, srows) - n_forced

        def pick(it, carry):
            sc = score_ref[0:srows, :]
            mx = jnp.max(sc, axis=0, keepdims=True)
            first = jnp.min(jnp.where(sc == mx, ji, srows), axis=0, keepdims=True)
            hit = ji == first
            sel_ref[0:srows, :] = jnp.where(hit & (mx >= 0.0) & (it < quota), 0.0, sel_ref[0:srows, :])
            score_ref[0:srows, :] = jnp.where(hit, BELOW_NEG, sc)
            return carry

        n_picks = min(N_SELECT, srows) - jnp.where(n == 0, 1, 3)
        lax.fori_loop(0, n_picks, pick, 0)

    per_q = Q_BLOCK // CMP_STRIDE
    classes = sorted({r for r in (nb // 4, nb // 2) if r % (SEL_RATIO * SUBLANES * 2) == 0} | {nb})
    for k, rows in enumerate(classes):
        lo = 0 if k == 0 else classes[k - 1] // per_q
        in_class = (n >= lo) if k == len(classes) - 1 else ((n >= lo) & (n < rows // per_q))
        pl.when(in_class)(functools.partial(compress_and_select, rows))

    sel = sel_ref[...]
    wq_ref[wq_ref.shape[0] - 1, 0:HEAD_DIM, :] = jnp.zeros((HEAD_DIM, nq), BF16)
    wq_ref[wq_ref.shape[0] - 1, HEAD_DIM:HEAD_DIM + LANES, :] = jnp.full((LANES, nq), NEG, BF16)
    for h in range(wq_ref.shape[0] - 1):
        rows_h = min(LANES, ns - LANES * h)
        wq_ref[h, 0:HEAD_DIM, :] = qc
        blk = sel[LANES * h:LANES * h + rows_h].astype(BF16)
        for r in range(HEADS_PER_GROUP):
            wq_ref[h, HEAD_DIM:HEAD_DIM + rows_h, r * Q_BLOCK:(r + 1) * Q_BLOCK] = blk
        if rows_h < LANES:
            wq_ref[h, HEAD_DIM + rows_h:HEAD_DIM + LANES, :] = jnp.zeros((LANES - rows_h, nq), BF16)

    m0 = jnp.full((1, nq), NEG, F32)

    aslc_ref[...] = jnp.zeros(aslc_ref.shape, F32)
    blocks_per_chunk = SLC_CHUNK // SEL_BLOCK

    ml_ref[0:1, :] = m0

    n_past = q_start // SLC_CHUNK
    last_chunk = ksl_ref.shape[2] // SLC_CHUNK - 1
    n_half = wq_ref.shape[0] - 1

    def slc_scores(c, buf_ref):
        slot = jnp.where(c < n_past, (c * blocks_per_chunk) // LANES, n_half)
        k0 = pl.multiple_of(jnp.minimum(c, last_chunk) * SLC_CHUNK, SLC_CHUNK)
        buf_ref[...] = _dot(ksl_ref[0, 0, pl.ds(k0, SLC_CHUNK), :], wq_ref[slot])

    def slc_flash(c, buf_ref):
        v_t = vslT_ref[0, 0, jnp.minimum(c, last_chunk)]
        ml_ref[0:1, :] = _flash_step(buf_ref[...], v_t, ml_ref[0:1, :], aslc_ref)

    def slc_round(i, carry):
        bufs = (sa_ref, sb_ref, sc_ref)
        for k in range(SLC_UNROLL):
            slc_scores(SLC_UNROLL * i + k + 2, bufs[(k + 2) % 3])
            slc_flash(SLC_UNROLL * i + k, bufs[k % 3])
        return carry

    slc_scores(0, sa_ref)
    slc_scores(1, sb_ref)
    lax.fori_loop(0, (n_past + SLC_UNROLL - 1) // SLC_UNROLL, slc_round, 0)

    k0 = pl.multiple_of(n_past * SLC_CHUNK, SLC_CHUNK)
    sd = _dot(ksl_ref[0, 0, pl.ds(k0, SLC_CHUNK), :], wq_ref[(n_past * blocks_per_chunk) // LANES])
    kpos = k0 + lax.broadcasted_iota(I32, (SLC_CHUNK, nq), 0)
    sd = jnp.where(kpos <= tok, sd, NEG)
    _flash_step(sd, vslT_ref[0, 0, n_past], ml_ref[0:1, :], aslc_ref)

    n_wc = WINDOW // WIN_CHUNK + 1
    kk = lax.broadcasted_iota(I32, (WIN_CHUNK, nq), 0)
    chunk_ids = []
    m_win = m0
    for i in range(n_wc):
        wb = n - (n_wc - 1) + i
        wbc = jnp.maximum(wb, 0)
        chunk_ids.append(wbc)
        k0 = pl.multiple_of(wbc * WIN_CHUNK, WIN_CHUNK)
        sc = _dot(kwin_ref[0, 0, pl.ds(k0, WIN_CHUNK), :], qc)
        if i == 0:
            sc = jnp.where(kk > tin, sc, NEG)
        if i == n_wc - 1:
            sc = jnp.where(kk <= tin, sc, NEG)
        else:
            sc = sc + jnp.where(wb >= 0, 0.0, NEG)
        wbuf_ref[i] = sc
        m_win = jnp.maximum(m_win, jnp.max(sc, axis=0, keepdims=True))
    for i in range(n_wc):
        p = jnp.exp2(wbuf_ref[i] - m_win)
        pv = _dot(vwinT_ref[0, 0, chunk_ids[i]], p.astype(BF16))
        if i == 0:
            awin_ref[...] = pv
        else:
            awin_ref[...] = awin_ref[...] + pv

    gates = gT_ref[0, 0]
    inv_slc = 1.0 / aslc_ref[HEAD_DIM:HEAD_DIM + 1, :]
    inv_win = 1.0 / awin_ref[HEAD_DIM:HEAD_DIM + 1, :]
    for r in range(HEADS_PER_GROUP):
        cols = slice(r * Q_BLOCK, (r + 1) * Q_BLOCK)
        o = (gates[3 * r:3 * r + 1] * ocmp_ref[:, cols]
             + gates[3 * r + 1:3 * r + 2] * inv_slc[:, cols] * aslc_ref[0:HEAD_DIM, cols]
             + gates[3 * r + 2:3 * r + 3] * inv_win[:, cols] * awin_ref[0:HEAD_DIM, cols])
        y_ref[0, :, r * HEAD_DIM:(r + 1) * HEAD_DIM] = o.T.astype(BF16)


def _nsa(qT, kc, kcT, ksl, vslT, kwin, vwinT, gT):
    b, _, s = qT.shape
    nb = kc.shape[2]
    ns = s // SEL_BLOCK
    g4 = HEADS_PER_GROUP * HEAD_DIM
    nq = HEADS_PER_GROUP * Q_BLOCK
    gates = gT.reshape(b, N_KV_GROUPS, HEADS_PER_GROUP * 3, s)
    per_bg = lambda bi, g, n: (bi, g, 0, 0)
    per_bg5 = lambda bi, g, n: (bi, g, 0, 0, 0)
    once = pl.Buffered(1)
    return pl.pallas_call(
        _nsa_kernel,
        grid=(b, N_KV_GROUPS, s // Q_BLOCK),
        in_specs=[
            pl.BlockSpec((1, g4, Q_BLOCK), lambda bi, g, n: (bi, g, n)),
            pl.BlockSpec((1, 1, nb, HEAD_DIM), per_bg, pipeline_mode=once),
            pl.BlockSpec((1, 1, HEAD_DIM, nb), lambda bi, g, n: (bi, N_KV_GROUPS + g, 0, 0), pipeline_mode=once),
            pl.BlockSpec((1, 1, s, 2 * HEAD_DIM), per_bg, pipeline_mode=once),
            pl.BlockSpec((1, 1, s // SLC_CHUNK, V_ROWS, SLC_CHUNK), per_bg5, pipeline_mode=once),
            pl.BlockSpec((1, 1, s, HEAD_DIM), per_bg, pipeline_mode=once),
            pl.BlockSpec((1, 1, s // WIN_CHUNK, V_ROWS, WIN_CHUNK), per_bg5, pipeline_mode=once),
            pl.BlockSpec((1, 1, HEADS_PER_GROUP * 3, Q_BLOCK), lambda bi, g, n: (bi, g, 0, n)),
        ],
        out_specs=pl.BlockSpec((1, Q_BLOCK, g4), lambda bi, g, n: (bi, n, g)),
        out_shape=jax.ShapeDtypeStruct((b, s, NSA_WIDTH), BF16),
        scratch_shapes=[
            pltpu.VMEM((HEAD_DIM, nq), BF16),
            pltpu.VMEM((nb + SUBLANES, Q_BLOCK), F32),
            pltpu.VMEM((ns, Q_BLOCK), F32),
            pltpu.VMEM((ns, Q_BLOCK), F32),
            pltpu.VMEM((pl.cdiv(ns, LANES) + 1, 2 * HEAD_DIM, nq), BF16),
            pltpu.VMEM((SLC_CHUNK, nq), F32),
            pltpu.VMEM((SLC_CHUNK, nq), F32),
            pltpu.VMEM((SLC_CHUNK, nq), F32),
            pltpu.VMEM((SUBLANES, nq), F32),
            pltpu.VMEM((WINDOW // WIN_CHUNK + 1, WIN_CHUNK, nq), F32),
            pltpu.VMEM((HEAD_DIM, nq), F32),
            pltpu.VMEM((V_ROWS, nq), F32),
            pltpu.VMEM((V_ROWS, nq), F32),
        ],
        compiler_params=_params("arbitrary", "arbitrary", "arbitrary"),
        name="nsa_attention",
    )(qT, kc, kcT, ksl, vslT, kwin, vwinT, gates)


def _outproj_kernel(yp_ref, yn_ref, wo_ref, x_ref, g1_ref, n2_ref, sc_ref, sh_ref, wrT_ref, br_ref, tri_ref,
                    x1_ref, h2_ref, idx_ref, wt_ref, rank_ref, cnt_ref, carry_ref):
    tm = x_ref.shape[1]
    n_exp = wrT_ref.shape[0]

    @pl.when((pl.program_id(0) == 0) & (pl.program_id(1) == 0))
    def _():
        carry_ref[...] = jnp.zeros(carry_ref.shape, F32)

    mix = _dot(yp_ref[0], wo_ref[0:POOL_WIDTH, :]) + _dot(yn_ref[0], wo_ref[POOL_WIDTH:POOL_WIDTH + NSA_WIDTH, :])
    x1 = x_ref[0] + g1_ref[0] * mix
    x1_ref[0] = x1
    h2 = _rmsnorm(x1, n2_ref[...]) * (1.0 + sc_ref[0]) + sh_ref[0]
    h2_ref[0] = h2

    logits = lax.dot_general(wrT_ref[...], h2.astype(BF16), (((1,), (1,)), ((), ())),
                             preferred_element_type=F32) + br_ref[...]
    ei = lax.broadcasted_iota(I32, (n_exp, tm), 0)
    vals, ids = [], []
    for _ in range(TOP_K):
        mx = jnp.max(logits, axis=0, keepdims=True)
        first = jnp.min(jnp.where(logits == mx, ei, n_exp), axis=0, keepdims=True)
        vals.append(mx)
        ids.append(first)
        logits = jnp.where(ei == first, -jnp.inf, logits)
    exps = [jnp.exp(v - vals[0]) for v in vals]
    den = exps[0]
    for e in exps[1:]:
        den = den + e

    onehot = jnp.zeros((n_exp, tm), F32)
    for first in ids:
        onehot = onehot + (ei == first).astype(F32)
    before = _dot(onehot.astype(BF16), tri_ref[...]) + carry_ref[:, 0:1]
    wt_ref[TOP_K:SUBLANES, :] = jnp.zeros((SUBLANES - TOP_K, tm), F32)
    for k in range(TOP_K):
        idx_ref[k:k + 1, :] = ids[k]
        wt_ref[k:k + 1, :] = exps[k] / den
        rank = jnp.sum(jnp.where(ei == ids[k], before, 0.0), axis=0, keepdims=True)
        rank_ref[k:k + 1, :] = rank.astype(I32)
    carry_ref[...] = carry_ref[...] + jnp.sum(onehot, axis=1, keepdims=True)
    cnt_ref[...] = carry_ref[...].astype(I32)


def _output_projection(y_pool, y_nsa, w_out, x, gate1, norm_g, scale, shift, w_router_t, b_router, tri):
    b, s, d = x.shape
    tm = PROJ_TM
    n_exp = w_router_t.shape[0]
    t_all = b * s
    tok = lambda bi, m: (bi, m, 0)
    per_b = lambda bi, m: (bi, 0, 0)
    const2 = lambda bi, m: (0, 0)
    flat = lambda bi, m: (0, bi * (s // tm) + m)
    return pl.pallas_call(
        _outproj_kernel,
        grid=(b, s // tm),
        in_specs=[
            pl.BlockSpec((1, tm, POOL_WIDTH), tok),
            pl.BlockSpec((1, tm, NSA_WIDTH), tok),
            pl.BlockSpec((POOL_WIDTH + NSA_WIDTH, d), const2, pipeline_mode=pl.Buffered(1)),
            pl.BlockSpec((1, tm, d), tok),
            pl.BlockSpec((1, 1, d), per_b),
            pl.BlockSpec((1, d), const2),
            pl.BlockSpec((1, 1, d), per_b),
            pl.BlockSpec((1, 1, d), per_b),
            pl.BlockSpec((n_exp, d), const2),
            pl.BlockSpec((n_exp, 1), const2),
            pl.BlockSpec((tm, tm), const2, pipeline_mode=pl.Buffered(1)),
        ],
        out_specs=(
            pl.BlockSpec((1, tm, d), tok),
            pl.BlockSpec((1, tm, d), tok),
            pl.BlockSpec((TOP_K, tm), flat),
            pl.BlockSpec((SUBLANES, tm), flat),
            pl.BlockSpec((TOP_K, tm), flat),
            pl.BlockSpec((n_exp, LANES), const2),
        ),
        out_shape=(
            jax.ShapeDtypeStruct((b, s, d), F32),
            jax.ShapeDtypeStruct((b, s, d), F32),
            jax.ShapeDtypeStruct((TOP_K, t_all), I32),
            jax.ShapeDtypeStruct((SUBLANES, t_all), F32),
            jax.ShapeDtypeStruct((TOP_K, t_all), I32),
            jax.ShapeDtypeStruct((n_exp, LANES), I32),
        ),
        scratch_shapes=[pltpu.VMEM((n_exp, LANES), F32)],
        compiler_params=_params("arbitrary", "arbitrary"),
        name="output_projection_router",
    )(y_pool, y_nsa, w_out, x, gate1, norm_g, scale, shift, w_router_t, b_router, tri)


def _dest_kernel(ps_ref, idx_ref, rank_ref, dest_ref):
    idx = idx_ref[...]
    dest = rank_ref[...]
    for e in range(ps_ref.shape[0]):
        dest = dest + jnp.where(idx == e, ps_ref[e], 0)
    dest_ref[...] = dest


def _destinations(seg_starts, idx_t, rank_t):
    return pl.pallas_call(
        _dest_kernel,
        in_specs=[
            pl.BlockSpec(memory_space=pltpu.SMEM),
            pl.BlockSpec(memory_space=pltpu.VMEM),
            pl.BlockSpec(memory_space=pltpu.VMEM),
        ],
        out_specs=pl.BlockSpec(memory_space=pltpu.VMEM),
        out_shape=jax.ShapeDtypeStruct(idx_t.shape, I32),
        name="moe_destinations",
    )(seg_starts, idx_t, rank_t)


def _row_copy(src_ref, src_row, dst_ref, dst_row, sem):
    return pltpu.make_async_copy(src_ref.at[pl.ds(src_row, 1)], dst_ref.at[pl.ds(dst_row, 1)], sem)


def _dispatch_kernel(dest_ref, starts_ref, ends_ref, h_ref, xs_ref, zero_ref, sem, zsem):
    tm = h_ref.shape[0]
    n_exp = starts_ref.shape[0]
    zrows = zero_ref.shape[0]

    @pl.when(pl.program_id(0) == 0)
    def _():
        zero_ref[...] = jnp.zeros(zero_ref.shape, zero_ref.dtype)

        def clear_block(end_row, start):
            for r in range(EXPERT_TM // zrows):
                row0 = pl.multiple_of(end_row - EXPERT_TM + r * zrows, zrows)
                cp = pltpu.make_async_copy(zero_ref, xs_ref.at[pl.ds(row0, zrows)], zsem)
                cp.start() if start else cp.wait()

        def clear(e, start):
            @pl.when(ends_ref[e] > starts_ref[e])
            def _():
                clear_block(ends_ref[e], start)

        first_unused = ends_ref[n_exp - 1] // EXPERT_TM
        n_blk = xs_ref.shape[0] // EXPERT_TM
        for start in (True, False):
            lax.fori_loop(0, n_exp, lambda e, c: (clear(e, start), c)[1], 0)
            lax.fori_loop(first_unused, n_blk, lambda i, c: (clear_block((i + 1) * EXPERT_TM, start), c)[1], 0)

    def issue(t, carry):
        for k in range(TOP_K):
            _row_copy(h_ref, t, xs_ref, dest_ref[k, t], sem).start()
        return carry

    lax.fori_loop(0, tm, issue, 0)

    def drain(t, carry):
        for k in range(TOP_K):
            _row_copy(h_ref, t, xs_ref, dest_ref[k, t], sem).wait()
        return carry

    lax.fori_loop(0, tm, drain, 0)


def _dispatch(dest_t, seg_starts, seg_ends, h_rows, n_buf):
    t_all, d = h_rows.shape
    tm = DISPATCH_TM
    return pl.pallas_call(
        _dispatch_kernel,
        grid=(t_all // tm,),
        in_specs=[
            pl.BlockSpec((TOP_K, tm), lambda i: (0, i), memory_space=pltpu.SMEM),
            pl.BlockSpec(memory_space=pltpu.SMEM),
            pl.BlockSpec(memory_space=pltpu.SMEM),
            pl.BlockSpec((tm, d), lambda i: (i, 0)),
        ],
        out_specs=pl.BlockSpec(memory_space=pl.ANY),
        out_shape=jax.ShapeDtypeStruct((n_buf, d), h_rows.dtype),
        scratch_shapes=[pltpu.VMEM((DISPATCH_ZERO_ROWS, d), h_rows.dtype),
                        pltpu.SemaphoreType.DMA(()), pltpu.SemaphoreType.DMA(())],
        compiler_params=_params("arbitrary"),
        name="moe_dispatch",
    )(dest_t, seg_starts, seg_ends, h_rows)


def _ffn_up_kernel(be_ref, nu_ref, xs_ref, wg_ref, bg_ref, wu_ref, bu_ref, h_ref, xb_ref):
    del be_ref
    i = pl.program_id(0)
    j = pl.program_id(1)
    used = i < nu_ref[0]

    @pl.when(used & (j == 0))
    def _():
        xb_ref[...] = xs_ref[...].astype(BF16)

    @pl.when(used)
    def _():
        xb = xb_ref[...]
        gl = jnp.minimum(_dot(xb, wg_ref[0, 0].astype(BF16)) + bg_ref[0, 0], SWIGLU_LIMIT)
        lin = jnp.clip(_dot(xb, wu_ref[0, 0].astype(BF16)) + bu_ref[0, 0], -SWIGLU_LIMIT, SWIGLU_LIMIT)
        h_ref[...] = (gl * jax.nn.sigmoid(SWIGLU_ALPHA * gl) * (lin + 1.0)).astype(BF16)

    @pl.when(jnp.logical_not(used))
    def _():
        h_ref[...] = jnp.zeros(h_ref.shape, BF16)


def _ffn_down_kernel(be_ref, nu_ref, h_ref, wd_ref, bd_ref, y_ref):
    del be_ref
    used = pl.program_id(0) < nu_ref[0]

    @pl.when(used)
    def _():
        y_ref[...] = _dot(h_ref[...], wd_ref[0, 0].astype(BF16)) + bd_ref[0, 0]

    @pl.when(jnp.logical_not(used))
    def _():
        y_ref[...] = jnp.zeros(y_ref.shape, F32)


def _expert_ffn(layer, block_expert, n_used, xs, w_gate, b_gate, w_up, b_up, w_down, b_down):
    n_buf, d = xs.shape
    depth, n_exp, _, f = w_gate.shape
    tm, tf = EXPERT_TM, EXPERT_TF
    nf = f // tf
    nd = d // tf
    n_blk = n_buf // tm

    def rows(i, j, be, nu):
        return (jnp.minimum(i, nu[0] - 1), 0)

    def cols(nj):
        return lambda i, j, be, nu: (layer, be[i], 0, jnp.where(i < nu[0], j, nj - 1))

    hidden = pl.pallas_call(
        _ffn_up_kernel,
        grid_spec=pltpu.PrefetchScalarGridSpec(
            num_scalar_prefetch=2,
            grid=(n_blk, nf),
            in_specs=[
                pl.BlockSpec((tm, d), rows),
                pl.BlockSpec((1, 1, d, tf), cols(nf)),
                pl.BlockSpec((1, 1, 1, tf), cols(nf)),
                pl.BlockSpec((1, 1, d, tf), cols(nf)),
                pl.BlockSpec((1, 1, 1, tf), cols(nf)),
            ],
            out_specs=pl.BlockSpec((tm, tf), lambda i, j, be, nu: (i, j)),
            scratch_shapes=[pltpu.VMEM((tm, d), BF16)],
        ),
        out_shape=jax.ShapeDtypeStruct((n_buf, f), BF16),
        compiler_params=_params("arbitrary", "arbitrary"),
        name="moe_ffn_up",
    )(block_expert, n_used, xs, w_gate, b_gate.reshape(depth, n_exp, 1, f), w_up, b_up.reshape(depth, n_exp, 1, f))
    return pl.pallas_call(
        _ffn_down_kernel,
        grid_spec=pltpu.PrefetchScalarGridSpec(
            num_scalar_prefetch=2,
            grid=(n_blk, nd),
            in_specs=[
                pl.BlockSpec((tm, f), rows),
                pl.BlockSpec((1, 1, f, tf), cols(nd)),
                pl.BlockSpec((1, 1, 1, tf), cols(nd)),
            ],
            out_specs=pl.BlockSpec((tm, tf), lambda i, j, be, nu: (i, j)),
        ),
        out_shape=jax.ShapeDtypeStruct((n_buf, d), F32),
        compiler_params=_params("arbitrary", "arbitrary"),
        name="moe_ffn_down",
    )(block_expert, n_used, hidden, w_down, b_down.reshape(depth, n_exp, 1, d))


def _combine_kernel(dest_ref, wt_ref, x1_ref, g2_ref, fg_ref, y_ref, o_ref, rows_ref, sem, *, final_norm):
    tm = x1_ref.shape[1]

    def issue(t, carry):
        for k in range(TOP_K):
            _row_copy(y_ref, dest_ref[k, t], rows_ref.at[k], t, sem).start()
        return carry

    lax.fori_loop(0, tm, issue, 0)

    wt = jnp.concatenate([wt_ref[...], jnp.zeros((tm - SUBLANES, tm), F32)], axis=0).T

    def drain(t, carry):
        for k in range(TOP_K):
            _row_copy(y_ref, dest_ref[k, t], rows_ref.at[k], t, sem).wait()
        return carry

    lax.fori_loop(0, tm, drain, 0)

    acc = wt[:, 0:1] * rows_ref[0]
    for k in range(1, TOP_K):
        acc = acc + wt[:, k:k + 1] * rows_ref[k]
    out = x1_ref[0] + g2_ref[0] * acc
    if final_norm:
        out = _rmsnorm(out, fg_ref[...])
    o_ref[0] = out


def _combine(dest_t, wt_t, x1, gate2, final_g, y_rows, final_norm):
    b, s, d = x1.shape
    tm = COMBINE_TM
    flat = lambda bi, m: (0, bi * (s // tm) + m)
    return pl.pallas_call(
        functools.partial(_combine_kernel, final_norm=final_norm),
        grid=(b, s // tm),
        in_specs=[
            pl.BlockSpec((TOP_K, tm), flat, memory_space=pltpu.SMEM),
            pl.BlockSpec((SUBLANES, tm), flat),
            pl.BlockSpec((1, tm, d), lambda bi, m: (bi, m, 0)),
            pl.BlockSpec((1, 1, d), lambda bi, m: (bi, 0, 0)),
            pl.BlockSpec((1, d), lambda bi, m: (0, 0)),
            pl.BlockSpec(memory_space=pl.ANY),
        ],
        out_specs=pl.BlockSpec((1, tm, d), lambda bi, m: (bi, m, 0)),
        out_shape=jax.ShapeDtypeStruct((b, s, d), F32),
        scratch_shapes=[pltpu.VMEM((TOP_K, tm, d), F32), pltpu.SemaphoreType.DMA(())],
        compiler_params=_params("arbitrary", "arbitrary"),
        name="moe_combine",
    )(dest_t, wt_t, x1, gate2, final_g, y_rows)


def _segment_layout(counts, n_blk):
    padded = (counts + EXPERT_TM - 1) // EXPERT_TM * EXPERT_TM
    ends = jnp.cumsum(padded)
    starts = ends - padded
    n_used = ends[-1] // EXPERT_TM
    blk = jnp.minimum(jnp.arange(n_blk, dtype=I32), n_used - 1) * EXPERT_TM
    block_expert = jnp.minimum(jnp.searchsorted(ends, blk, side="right"), counts.shape[0] - 1)
    return starts.astype(I32), ends.astype(I32), block_expert.astype(I32), n_used.reshape(1).astype(I32)


def kernel(x, c, norm1_g, norm2_g, w_ada, b_ada, w_in, pool_w, pool_scale, cmp_pos_k, cmp_w1_k, cmp_b1_k, cmp_w2_k, cmp_b2_k, cmp_pos_v, cmp_w1_v, cmp_b1_v, cmp_w2_v, cmp_b2_v, w_out, w_router, b_router, w_gate, b_gate, w_up, b_up, w_down, b_down, final_g):
    b, s, d = x.shape
    depth = w_ada.shape[0]
    n_exp = w_router.shape[-1]
    t_all = b * s
    assert s % PROJ_TM == 0 and s % SLC_CHUNK == 0 and t_all % DISPATCH_TM == 0
    n_main = POOL_WIDTH + NSA_WIDTH + 6 * KV_WIDTH
    assert w_in.shape[-1] == n_main + N_GATE_COLS

    c_pad = jnp.pad(c, ((0, SUBLANES - b), (0, 0)))
    mod = _modulation(c_pad, w_ada, b_ada)[:, :b]
    tri = (jnp.arange(PROJ_TM)[:, None] < jnp.arange(PROJ_TM)[None, :]).astype(BF16)
    n_buf = (t_all * TOP_K + n_exp * (EXPERT_TM - 1)) // EXPERT_TM * EXPERT_TM
    n_blk = n_buf // EXPERT_TM

    for l in range(depth):
        sh1, sc1, g1, sh2, sc2, g2 = [m.reshape(b, 1, d) for m in jnp.split(mod[l], 6, axis=-1)]
        w_main = w_in[l, :, :n_main].astype(BF16)
        w_gates = jnp.pad(w_in[l, :, n_main:], ((0, 0), (0, LANES - N_GATE_COLS))).astype(BF16)
        u_pool, qT, kvc, ksl, vslT, kwin, vwinT, gT = _input_projection(
            x, norm1_g[l].reshape(1, d), sc1, sh1, w_main, w_gates)
        y_pool = _pool_mixer(u_pool, pool_w[l].astype(BF16), pool_scale[l].reshape(1, POOL_WIDTH))
        kc, kcT = _compress(
            kvc,
            jnp.stack([cmp_pos_k[l], cmp_pos_v[l]]).reshape(2, 1, CMP_BLOCK * HEAD_DIM),
            jnp.stack([cmp_w1_k[l], cmp_w1_v[l]]).astype(BF16),
            jnp.stack([cmp_b1_k[l], cmp_b1_v[l]])[:, None, :],
            jnp.stack([cmp_w2_k[l], cmp_w2_v[l]]).astype(BF16),
            jnp.stack([cmp_b2_k[l], cmp_b2_v[l]])[:, None, :])
        y_nsa = _nsa(qT, kc, kcT, ksl, vslT, kwin, vwinT, gT)
        x1, h2, idx_t, wt_t, rank_t, counts = _output_projection(
            y_pool, y_nsa, w_out[l].astype(BF16), x, g1, norm2_g[l].reshape(1, d), sc2, sh2,
            w_router[l].T.astype(BF16), b_router[l].reshape(n_exp, 1), tri)
        seg_starts, seg_ends, block_expert, n_used = _segment_layout(counts[:, 0], n_blk)
        dest_t = _destinations(seg_starts, idx_t, rank_t)
        xs = _dispatch(dest_t, seg_starts, seg_ends, h2.reshape(t_all, d), n_buf)
        y_rows = _expert_ffn(l, block_expert, n_used, xs, w_gate, b_gate, w_up, b_up, w_down, b_down)
        x = _combine(dest_t, wt_t, x1, g2, final_g.reshape(1, d), y_rows, final_norm=(l == depth - 1))
    return x
```

```python
import functools

import jax
import jax.numpy as jnp
from jax import lax
from jax.experimental import pallas as pl
from jax.experimental.pallas import tpu as pltpu

F32 = jnp.float32
BF16 = jnp.bfloat16
I32 = jnp.int32

HEAD_DIM = 128
N_HEADS = 8
N_KV_GROUPS = 2
HEADS_PER_GROUP = N_HEADS // N_KV_GROUPS
POOL_WINDOWS = (2, 4, 8, 16)
POOL_GROUP = 256
POOL_WIDTH = POOL_GROUP * len(POOL_WINDOWS)
NSA_WIDTH = N_HEADS * HEAD_DIM
KV_WIDTH = N_KV_GROUPS * HEAD_DIM
N_GATE_COLS = 3 * N_HEADS
CMP_BLOCK = 32
CMP_STRIDE = 16
SEL_BLOCK = 64
SEL_RATIO = SEL_BLOCK // CMP_STRIDE
N_SELECT = 16
WINDOW = 512
Q_BLOCK = 128
TOP_K = 4
SWIGLU_ALPHA = 1.702
SWIGLU_LIMIT = 7.0
NORM_EPS = 1e-5
NEG = -1e30
BIG = 1e30
BELOW_NEG = -3e38
V_ROWS = HEAD_DIM + 16
LOG2_E = 1.4426950408889634

LANES = 128
SUBLANES = 8
VMEM_LIMIT_BYTES = 56 * 1024 * 1024

MOD_TN = 1024
PROJ_TM = 512
SLC_CHUNK = 256
SLC_UNROLL = 6
WIN_CHUNK = 128
DISPATCH_TM = 256
DISPATCH_ZERO_ROWS = 256
COMBINE_TM = 128
EXPERT_TM = 1024
EXPERT_TF = 512


def _params(*sem):
    return pltpu.CompilerParams(dimension_semantics=sem, vmem_limit_bytes=VMEM_LIMIT_BYTES)


def _dot(a, b):
    return jnp.dot(a, b, preferred_element_type=F32)


def _rmsnorm(x, g):
    return x * lax.rsqrt(jnp.mean(x * x, axis=-1, keepdims=True) + NORM_EPS) * g


_HIGH_HALF = 0xFFFF0000


def _pack_bf16_pair(a, b):
    hi = pltpu.bitcast(a.astype(BF16).astype(F32), jnp.uint32) & jnp.uint32(_HIGH_HALF)
    lo = pltpu.bitcast(b.astype(BF16).astype(F32), jnp.uint32) >> 16
    return hi | lo


def _unpack_bf16_pair(w):
    return pltpu.bitcast(w & jnp.uint32(_HIGH_HALF), F32), pltpu.bitcast(w << 16, F32)


def _mod_kernel(c_ref, w_ref, b_ref, o_ref):
    c = c_ref[...]
    ca = (c * jax.nn.sigmoid(c)).astype(BF16)
    o_ref[0] = _dot(ca, w_ref[0].astype(BF16)) + b_ref[0]


def _modulation(c_pad, w_ada, b_ada):
    depth, d, n = w_ada.shape
    rows = c_pad.shape[0]
    return pl.pallas_call(
        _mod_kernel,
        grid=(depth, n // MOD_TN),
        in_specs=[
            pl.BlockSpec((rows, d), lambda l, j: (0, 0)),
            pl.BlockSpec((1, d, MOD_TN), lambda l, j: (l, 0, j)),
            pl.BlockSpec((1, 1, MOD_TN), lambda l, j: (l, 0, j)),
        ],
        out_specs=pl.BlockSpec((1, rows, MOD_TN), lambda l, j: (l, 0, j)),
        out_shape=jax.ShapeDtypeStruct((depth, rows, n), F32),
        compiler_params=_params("arbitrary", "arbitrary"),
        name="adaln_modulation",
    )(c_pad, w_ada, b_ada.reshape(depth, 1, n))


def _inproj_kernel(x_ref, g_ref, sc_ref, sh_ref, w_ref, wg_ref,
                   upool_ref, qT_ref, kvc_ref, ksl_ref, vslT_ref, kwin_ref, vwinT_ref, gT_ref):
    tm = x_ref.shape[1]
    h = _rmsnorm(x_ref[0], g_ref[...]) * (1.0 + sc_ref[0]) + sh_ref[0]
    hb = h.astype(BF16)
    upool_ref[0] = _dot(hb, w_ref[:, 0:POOL_WIDTH])
    q0 = POOL_WIDTH
    scale = HEAD_DIM ** -0.5 * LOG2_E
    def four_heads(col0):
        wide = _dot(hb, w_ref[:, col0:col0 + 4 * HEAD_DIM])
        return [wide[:, i * HEAD_DIM:(i + 1) * HEAD_DIM] for i in range(4)]

    for j in range(N_HEADS // 4):
        for i, q in enumerate(four_heads(q0 + j * 4 * HEAD_DIM)):
            hd = 4 * j + i
            qT_ref[0, hd * HEAD_DIM:(hd + 1) * HEAD_DIM, :] = (q * scale).T.astype(BF16)
    c0 = q0 + NSA_WIDTH
    for i, kv in enumerate(four_heads(c0)):
        kvc_ref[0, i] = kv
    s0 = c0 + 2 * KV_WIDTH
    w0 = s0 + 2 * KV_WIDTH
    slc = four_heads(s0)
    win = four_heads(w0)
    tpos = pl.program_id(1) * tm + lax.broadcasted_iota(I32, (tm, HEAD_DIM), 0)
    lane = lax.broadcasted_iota(I32, (tm, HEAD_DIM), 1)
    block_onehot = jnp.where(lane == ((tpos // SEL_BLOCK) & (LANES - 1)), 1.0, 0.0).astype(BF16)
    for g in range(N_KV_GROUPS):
        ksl_ref[0, g, :, 0:HEAD_DIM] = slc[g].astype(BF16)
        ksl_ref[0, g, :, HEAD_DIM:2 * HEAD_DIM] = block_onehot
        vT = slc[N_KV_GROUPS + g].T.astype(BF16)
        for c in range(tm // SLC_CHUNK):
            vslT_ref[0, g, c, 0:HEAD_DIM, :] = vT[:, c * SLC_CHUNK:(c + 1) * SLC_CHUNK]
            vslT_ref[0, g, c, HEAD_DIM:V_ROWS, :] = jnp.ones((V_ROWS - HEAD_DIM, SLC_CHUNK), BF16)
        kwin_ref[0, g] = win[g].astype(BF16)
        vT = win[N_KV_GROUPS + g].T.astype(BF16)
        for c in range(tm // WIN_CHUNK):
            vwinT_ref[0, g, c, 0:HEAD_DIM, :] = vT[:, c * WIN_CHUNK:(c + 1) * WIN_CHUNK]
            vwinT_ref[0, g, c, HEAD_DIM:V_ROWS, :] = jnp.ones((V_ROWS - HEAD_DIM, WIN_CHUNK), BF16)
    gl = jax.nn.sigmoid(_dot(hb, wg_ref[...]))
    gT_ref[0] = gl.T[0:N_GATE_COLS, :]


def _input_projection(x, norm_g, scale, shift, w_main, w_gates):
    b, s, d = x.shape
    tm = PROJ_TM
    n_main = w_main.shape[1]
    tok = lambda bi, m: (bi, m, 0)
    per_b = lambda bi, m: (bi, 0, 0)
    const2 = lambda bi, m: (0, 0)
    out_shape = (
        jax.ShapeDtypeStruct((b, s, POOL_WIDTH), F32),
        jax.ShapeDtypeStruct((b, NSA_WIDTH, s), BF16),
        jax.ShapeDtypeStruct((b, 2 * N_KV_GROUPS, s, HEAD_DIM), F32),
        jax.ShapeDtypeStruct((b, N_KV_GROUPS, s, 2 * HEAD_DIM), BF16),
        jax.ShapeDtypeStruct((b, N_KV_GROUPS, s // SLC_CHUNK, V_ROWS, SLC_CHUNK), BF16),
        jax.ShapeDtypeStruct((b, N_KV_GROUPS, s, HEAD_DIM), BF16),
        jax.ShapeDtypeStruct((b, N_KV_GROUPS, s // WIN_CHUNK, V_ROWS, WIN_CHUNK), BF16),
        jax.ShapeDtypeStruct((b, N_GATE_COLS, s), F32),
    )
    out_specs = (
        pl.BlockSpec((1, tm, POOL_WIDTH), tok),
        pl.BlockSpec((1, NSA_WIDTH, tm), lambda bi, m: (bi, 0, m)),
        pl.BlockSpec((1, 2 * N_KV_GROUPS, tm, HEAD_DIM), lambda bi, m: (bi, 0, m, 0)),
        pl.BlockSpec((1, N_KV_GROUPS, tm, 2 * HEAD_DIM), lambda bi, m: (bi, 0, m, 0)),
        pl.BlockSpec((1, N_KV_GROUPS, tm // SLC_CHUNK, V_ROWS, SLC_CHUNK), lambda bi, m: (bi, 0, m, 0, 0)),
        pl.BlockSpec((1, N_KV_GROUPS, tm, HEAD_DIM), lambda bi, m: (bi, 0, m, 0)),
        pl.BlockSpec((1, N_KV_GROUPS, tm // WIN_CHUNK, V_ROWS, WIN_CHUNK), lambda bi, m: (bi, 0, m, 0, 0)),
        pl.BlockSpec((1, N_GATE_COLS, tm), lambda bi, m: (bi, 0, m)),
    )
    return pl.pallas_call(
        _inproj_kernel,
        grid=(b, s // tm),
        in_specs=[
            pl.BlockSpec((1, tm, d), tok),
            pl.BlockSpec((1, d), const2),
            pl.BlockSpec((1, 1, d), per_b),
            pl.BlockSpec((1, 1, d), per_b),
            pl.BlockSpec((d, n_main), const2, pipeline_mode=pl.Buffered(1)),
            pl.BlockSpec((d, LANES), const2, pipeline_mode=pl.Buffered(1)),
        ],
        out_specs=out_specs,
        out_shape=out_shape,
        compiler_params=_params("arbitrary", "arbitrary"),
        name="input_projection",
    )(x, norm_g, scale, shift, w_main, w_gates)


def _pool_kernel(u_ref, pw_ref, ps_ref, o_ref, ext_ref):
    tm = u_ref.shape[1]
    halo = max(POOL_WINDOWS)
    m = pl.program_id(1)

    @pl.when(m == 0)
    def _():
        ext_ref[0:halo, :] = jnp.zeros((halo, POOL_WIDTH), F32)

    ext_ref[halo:halo + tm, :] = u_ref[0]
    t = m * tm + lax.broadcasted_iota(I32, (tm, 1), 0)
    for gi, w in enumerate(POOL_WINDOWS):
        cols = slice(gi * POOL_GROUP, (gi + 1) * POOL_GROUP)
        ug = ext_ref[halo:halo + tm, cols]
        acc = ug
        for i in range(1, w):
            acc = acc + ext_ref[halo - i:halo - i + tm, cols]
        cnt = jnp.minimum(t + 1, w).astype(F32)
        diff = (acc / cnt - ug).astype(BF16)
        o_ref[0, :, cols] = (_dot(diff, pw_ref[gi]) * ps_ref[:, cols]).astype(BF16)
    ext_ref[0:halo, :] = ext_ref[tm:tm + halo, :]


def _pool_mixer(u_pool, pool_w, pool_scale):
    b, s, _ = u_pool.shape
    tm = PROJ_TM
    return pl.pallas_call(
        _pool_kernel,
        grid=(b, s // tm),
        in_specs=[
            pl.BlockSpec((1, tm, POOL_WIDTH), lambda bi, m: (bi, m, 0)),
            pl.BlockSpec((len(POOL_WINDOWS), POOL_GROUP, POOL_GROUP), lambda bi, m: (0, 0, 0)),
            pl.BlockSpec((1, POOL_WIDTH), lambda bi, m: (0, 0)),
        ],
        out_specs=pl.BlockSpec((1, tm, POOL_WIDTH), lambda bi, m: (bi, m, 0)),
        out_shape=jax.ShapeDtypeStruct((b, s, POOL_WIDTH), BF16),
        scratch_shapes=[pltpu.VMEM((tm + max(POOL_WINDOWS), POOL_WIDTH), F32)],
        compiler_params=_params("arbitrary", "arbitrary"),
        name="pool_mixer",
    )(u_pool, pool_w, pool_scale)


def _compress_kernel(a_ref, pos_ref, w1_ref, b1_ref, w2_ref, b2_ref, o_ref, oT_ref):
    half = CMP_STRIDE * HEAD_DIM
    a = a_ref[0, 0]
    nb = a.shape[0]
    pos = pos_ref[0]
    a_lo = (a + pos[:, 0:half]).astype(BF16)
    a_hi = (a + pos[:, half:2 * half]).astype(BF16)
    p_lo = _dot(a_lo, w1_ref[0, 0:half, :])
    p_hi = _dot(a_hi, w1_ref[0, half:2 * half, :])
    hid = jax.nn.gelu(p_lo + pltpu.roll(p_hi, nb - 1, 0) + b1_ref[0])
    out = _dot(hid.astype(BF16), w2_ref[0]) + b2_ref[0]
    o_ref[0, 0] = out.astype(BF16)
    oT_ref[0, 0] = out.T.astype(BF16)


def _compress(kvc, pos, w1, b1, w2, b2):
    b, n4, s, _ = kvc.shape
    nb = s // CMP_STRIDE
    a = kvc.reshape(b, n4, nb, CMP_STRIDE * HEAD_DIM)
    hidden = w1.shape[-1]
    kv = lambda bi, i: (i // N_KV_GROUPS, 0, 0)
    return pl.pallas_call(
        _compress_kernel,
        grid=(b, n4),
        in_specs=[
            pl.BlockSpec((1, 1, nb, CMP_STRIDE * HEAD_DIM), lambda bi, i: (bi, i, 0, 0)),
            pl.BlockSpec((1, 1, CMP_BLOCK * HEAD_DIM), kv),
            pl.BlockSpec((1, CMP_BLOCK * HEAD_DIM, hidden), kv),
            pl.BlockSpec((1, 1, hidden), kv),
            pl.BlockSpec((1, hidden, HEAD_DIM), kv),
            pl.BlockSpec((1, 1, HEAD_DIM), kv),
        ],
        out_specs=(
            pl.BlockSpec((1, 1, nb, HEAD_DIM), lambda bi, i: (bi, i, 0, 0)),
            pl.BlockSpec((1, 1, HEAD_DIM, nb), lambda bi, i: (bi, i, 0, 0)),
        ),
        out_shape=(
            jax.ShapeDtypeStruct((b, n4, nb, HEAD_DIM), BF16),
            jax.ShapeDtypeStruct((b, n4, HEAD_DIM, nb), BF16),
        ),
        compiler_params=_params("arbitrary", "arbitrary"),
        name="kv_compress",
    )(a, pos, w1, b1, w2, b2)


def _flash_step(s, v_t, m_old, acc_ref):
    m_new = jnp.maximum(m_old, jnp.max(s, axis=0, keepdims=True))
    alpha = jnp.exp2(m_old - m_new)
    p = jnp.exp2(s - m_new)
    acc_ref[...] = acc_ref[...] * alpha + _dot(v_t, p.astype(BF16))
    return m_new


def _nsa_kernel(qT_ref, kc_ref, vcT_ref, ksl_ref, vslT_ref, kwin_ref, vwinT_ref, gT_ref, y_ref,
                qcat_ref, pbuf_ref, score_ref, sel_ref, wq_ref, sa_ref, sb_ref, sc_ref, ml_ref, wbuf_ref,
                ocmp_ref, aslc_ref, awin_ref):
    n = pl.program_id(2)
    q_start = n * Q_BLOCK
    nq = HEADS_PER_GROUP * Q_BLOCK
    nb = kc_ref.shape[2]
    ns = sel_ref.shape[0]

    for r in range(HEADS_PER_GROUP):
        qcat_ref[:, r * Q_BLOCK:(r + 1) * Q_BLOCK] = qT_ref[0, r * HEAD_DIM:(r + 1) * HEAD_DIM, :]
    qc = qcat_ref[...]
    tin = lax.broadcasted_iota(I32, (1, nq), 1) & (Q_BLOCK - 1)
    tok = q_start + tin

    sel_ref[...] = jnp.full((ns, Q_BLOCK), NEG, F32)
    t1 = q_start + lax.broadcasted_iota(I32, (1, Q_BLOCK), 1)
    tj = t1 // SEL_BLOCK

    def compress_and_select(rows):
        srows = rows // SEL_RATIO
        s = _dot(kc_ref[0, 0, 0:rows, :], qc)
        ci = lax.broadcasted_iota(I32, (rows, nq), 0)
        s = jnp.where(ci * CMP_STRIDE + (CMP_BLOCK - 1) <= tok, s, NEG)
        m = jnp.max(s, axis=0, keepdims=True)
        e = jnp.exp2(s - m)
        l = jnp.sum(e, axis=0, keepdims=True)
        inv = jnp.where(tok >= CMP_BLOCK - 1, 1.0 / l, 0.0)
        p = e * inv
        ocmp_ref[...] = _dot(vcT_ref[0, 0, :, 0:rows], p.astype(BF16))

        psum = p[:, 0:Q_BLOCK]
        for r in range(1, HEADS_PER_GROUP):
            psum = psum + p[:, r * Q_BLOCK:(r + 1) * Q_BLOCK]
        pbuf_ref[0:SUBLANES, :] = jnp.zeros((SUBLANES, Q_BLOCK), F32)
        pbuf_ref[SUBLANES:SUBLANES + rows, :] = psum
        imp = pbuf_ref[pl.ds(SUBLANES - 1, srows, stride=SEL_RATIO), :]
        for d in range(SEL_RATIO):
            imp = imp + pbuf_ref[pl.ds(SUBLANES + d, srows, stride=SEL_RATIO), :]

        ji = lax.broadcasted_iota(I32, (srows, Q_BLOCK), 0)
        forced = (ji == 0) | (ji == tj) | (ji == tj - 1)
        sel_ref[0:srows, :] = jnp.where(forced, 0.0, NEG)
        score_ref[0:srows, :] = jnp.where(forced, BELOW_NEG, jnp.where(ji * SEL_BLOCK <= t1, imp, NEG))
        n_forced = 1 + (tj > 0).astype(I32) + (tj > 1).astype(I32)
        quota = min(N_SELECT, srows) - n_forced

        def pick(it, carry):
            sc = score_ref[0:srows, :]
            mx = jnp.max(sc, axis=0, keepdims=True)
            first = jnp.min(jnp.where(sc == mx, ji, srows), axis=0, keepdims=True)
            hit = ji == first
            sel_ref[0:srows, :] = jnp.where(hit & (mx >= 0.0) & (it < quota), 0.0, sel_ref[0:srows, :])
            score_ref[0:srows, :] = jnp.where(hit, BELOW_NEG, sc)
            return carry

        n_picks = min(N_SELECT, srows) - jnp.where(n == 0, 1, 3)
        lax.fori_loop(0, n_picks, pick, 0)

    per_q = Q_BLOCK // CMP_STRIDE
    classes = sorted({r for r in (nb // 4, nb // 2) if r % (SEL_RATIO * SUBLANES * 2) == 0} | {nb})
    for k, rows in enumerate(classes):
        lo = 0 if k == 0 else classes[k - 1] // per_q
        in_class = (n >= lo) if k == len(classes) - 1 else ((n >= lo) & (n < rows // per_q))
        pl.when(in_class)(functools.partial(compress_and_select, rows))

    sel = sel_ref[...]
    wq_ref[wq_ref.shape[0] - 1, 0:HEAD_DIM, :] = jnp.zeros((HEAD_DIM, nq), BF16)
    wq_ref[wq_ref.shape[0] - 1, HEAD_DIM:HEAD_DIM + LANES, :] = jnp.full((LANES, nq), NEG, BF16)
    for h in range(wq_ref.shape[0] - 1):
        rows_h = min(LANES, ns - LANES * h)
        wq_ref[h, 0:HEAD_DIM, :] = qc
        blk = sel[LANES * h:LANES * h + rows_h].astype(BF16)
        for r in range(HEADS_PER_GROUP):
            wq_ref[h, HEAD_DIM:HEAD_DIM + rows_h, r * Q_BLOCK:(r + 1) * Q_BLOCK] = blk
        if rows_h < LANES:
            wq_ref[h, HEAD_DIM + rows_h:HEAD_DIM + LANES, :] = jnp.zeros((LANES - rows_h, nq), BF16)

    m0 = jnp.full((1, nq), NEG, F32)

    aslc_ref[...] = jnp.zeros(aslc_ref.shape, F32)
    blocks_per_chunk = SLC_CHUNK // SEL_BLOCK

    ml_ref[0:1, :] = m0

    n_past = q_start // SLC_CHUNK
    last_chunk = ksl_ref.shape[2] // SLC_CHUNK - 1
    n_half = wq_ref.shape[0] - 1

    def slc_scores(c, buf_ref):
        slot = jnp.where(c < n_past, (c * blocks_per_chunk) // LANES, n_half)
        k0 = pl.multiple_of(jnp.minimum(c, last_chunk) * SLC_CHUNK, SLC_CHUNK)
        buf_ref[...] = _dot(ksl_ref[0, 0, pl.ds(k0, SLC_CHUNK), :], wq_ref[slot])

    def slc_flash(c, buf_ref):
        v_t = vslT_ref[0, 0, jnp.minimum(c, last_chunk)]
        ml_ref[0:1, :] = _flash_step(buf_ref[...], v_t, ml_ref[0:1, :], aslc_ref)

    def slc_round(i, carry):
        bufs = (sa_ref, sb_ref, sc_ref)
        for k in range(SLC_UNROLL):
            slc_scores(SLC_UNROLL * i + k + 2, bufs[(k + 2) % 3])
            slc_flash(SLC_UNROLL * i + k, bufs[k % 3])
        return carry

    slc_scores(0, sa_ref)
    slc_scores(1, sb_ref)
    lax.fori_loop(0, (n_past + SLC_UNROLL - 1) // SLC_UNROLL, slc_round, 0)

    k0 = pl.multiple_of(n_past * SLC_CHUNK, SLC_CHUNK)
    sd = _dot(ksl_ref[0, 0, pl.ds(k0, SLC_CHUNK), :], wq_ref[(n_past * blocks_per_chunk) // LANES])
    kpos = k0 + lax.broadcasted_iota(I32, (SLC_CHUNK, nq), 0)
    sd = jnp.where(kpos <= tok, sd, NEG)
    _flash_step(sd, vslT_ref[0, 0, n_past], ml_ref[0:1, :], aslc_ref)

    n_wc = WINDOW // WIN_CHUNK + 1
    kk = lax.broadcasted_iota(I32, (WIN_CHUNK, nq), 0)
    chunk_ids = []
    m_win = m0
    for i in range(n_wc):
        wb = n - (n_wc - 1) + i
        wbc = jnp.maximum(wb, 0)
        chunk_ids.append(wbc)
        k0 = pl.multiple_of(wbc * WIN_CHUNK, WIN_CHUNK)
        sc = _dot(kwin_ref[0, 0, pl.ds(k0, WIN_CHUNK), :], qc)
        if i == 0:
            sc = jnp.where(kk > tin, sc, NEG)
        if i == n_wc - 1:
            sc = jnp.where(kk <= tin, sc, NEG)
        else:
            sc = sc + jnp.where(wb >= 0, 0.0, NEG)
        wbuf_ref[i] = sc
        m_win = jnp.maximum(m_win, jnp.max(sc, axis=0, keepdims=True))
    for i in range(n_wc):
        p = jnp.exp2(wbuf_ref[i] - m_win)
        pv = _dot(vwinT_ref[0, 0, chunk_ids[i]], p.astype(BF16))
        if i == 0:
            awin_ref[...] = pv
        else:
            awin_ref[...] = awin_ref[...] + pv

    gates = gT_ref[0, 0]
    inv_slc = 1.0 / aslc_ref[HEAD_DIM:HEAD_DIM + 1, :]
    inv_win = 1.0 / awin_ref[HEAD_DIM:HEAD_DIM + 1, :]
    for r in range(HEADS_PER_GROUP):
        cols = slice(r * Q_BLOCK, (r + 1) * Q_BLOCK)
        o = (gates[3 * r:3 * r + 1] * ocmp_ref[:, cols]
             + gates[3 * r + 1:3 * r + 2] * inv_slc[:, cols] * aslc_ref[0:HEAD_DIM, cols]
             + gates[3 * r + 2:3 * r + 3] * inv_win[:, cols] * awin_ref[0:HEAD_DIM, cols])
        y_ref[0, :, r * HEAD_DIM:(r + 1) * HEAD_DIM] = o.T.astype(BF16)


def _nsa(qT, kc, kcT, ksl, vslT, kwin, vwinT, gT):
    b, _, s = qT.shape
    nb = kc.shape[2]
    ns = s // SEL_BLOCK
    g4 = HEADS_PER_GROUP * HEAD_DIM
    nq = HEADS_PER_GROUP * Q_BLOCK
    gates = gT.reshape(b, N_KV_GROUPS, HEADS_PER_GROUP * 3, s)
    per_bg = lambda bi, g, n: (bi, g, 0, 0)
    per_bg5 = lambda bi, g, n: (bi, g, 0, 0, 0)
    once = pl.Buffered(1)
    return pl.pallas_call(
        _nsa_kernel,
        grid=(b, N_KV_GROUPS, s // Q_BLOCK),
        in_specs=[
            pl.BlockSpec((1, g4, Q_BLOCK), lambda bi, g, n: (bi, g, n)),
            pl.BlockSpec((1, 1, nb, HEAD_DIM), per_bg, pipeline_mode=once),
            pl.BlockSpec((1, 1, HEAD_DIM, nb), lambda bi, g, n: (bi, N_KV_GROUPS + g, 0, 0), pipeline_mode=once),
            pl.BlockSpec((1, 1, s, 2 * HEAD_DIM), per_bg, pipeline_mode=once),
            pl.BlockSpec((1, 1, s // SLC_CHUNK, V_ROWS, SLC_CHUNK), per_bg5, pipeline_mode=once),
            pl.BlockSpec((1, 1, s, HEAD_DIM), per_bg, pipeline_mode=once),
            pl.BlockSpec((1, 1, s // WIN_CHUNK, V_ROWS, WIN_CHUNK), per_bg5, pipeline_mode=once),
            pl.BlockSpec((1, 1, HEADS_PER_GROUP * 3, Q_BLOCK), lambda bi, g, n: (bi, g, 0, n)),
        ],
        out_specs=pl.BlockSpec((1, Q_BLOCK, g4), lambda bi, g, n: (bi, n, g)),
        out_shape=jax.ShapeDtypeStruct((b, s, NSA_WIDTH), BF16),
        scratch_shapes=[
            pltpu.VMEM((HEAD_DIM, nq), BF16),
            pltpu.VMEM((nb + SUBLANES, Q_BLOCK), F32),
            pltpu.VMEM((ns, Q_BLOCK), F32),
            pltpu.VMEM((ns, Q_BLOCK), F32),
            pltpu.VMEM((pl.cdiv(ns, LANES) + 1, 2 * HEAD_DIM, nq), BF16),
            pltpu.VMEM((SLC_CHUNK, nq), F32),
            pltpu.VMEM((SLC_CHUNK, nq), F32),
            pltpu.VMEM((SLC_CHUNK, nq), F32),
            pltpu.VMEM((SUBLANES, nq), F32),
            pltpu.VMEM((WINDOW // WIN_CHUNK + 1, WIN_CHUNK, nq), F32),
            pltpu.VMEM((HEAD_DIM, nq), F32),
            pltpu.VMEM((V_ROWS, nq), F32),
            pltpu.VMEM((V_ROWS, nq), F32),
        ],
        compiler_params=_params("arbitrary", "arbitrary", "arbitrary"),
        name="nsa_attention",
    )(qT, kc, kcT, ksl, vslT, kwin, vwinT, gates)


def _outproj_kernel(yp_ref, yn_ref, wo_ref, x_ref, g1_ref, n2_ref, sc_ref, sh_ref, wrT_ref, br_ref, tri_ref,
                    x1_ref, h2_ref, idx_ref, wt_ref, rank_ref, cnt_ref, carry_ref):
    tm = x_ref.shape[1]
    n_exp = wrT_ref.shape[0]

    @pl.when((pl.program_id(0) == 0) & (pl.program_id(1) == 0))
    def _():
        carry_ref[...] = jnp.zeros(carry_ref.shape, F32)

    mix = _dot(yp_ref[0], wo_ref[0:POOL_WIDTH, :]) + _dot(yn_ref[0], wo_ref[POOL_WIDTH:POOL_WIDTH + NSA_WIDTH, :])
    x1 = x_ref[0] + g1_ref[0] * mix
    x1_ref[0] = x1
    h2 = _rmsnorm(x1, n2_ref[...]) * (1.0 + sc_ref[0]) + sh_ref[0]
    half = h2.shape[1] // 2
    h2_ref[0] = _pack_bf16_pair(h2[:, 0:half], h2[:, half:2 * half])

    logits = lax.dot_general(wrT_ref[...], h2.astype(BF16), (((1,), (1,)), ((), ())),
                             preferred_element_type=F32) + br_ref[...]
    ei = lax.broadcasted_iota(I32, (n_exp, tm), 0)
    vals, ids = [], []
    for _ in range(TOP_K):
        mx = jnp.max(logits, axis=0, keepdims=True)
        first = jnp.min(jnp.where(logits == mx, ei, n_exp), axis=0, keepdims=True)
        vals.append(mx)
        ids.append(first)
        logits = jnp.where(ei == first, -jnp.inf, logits)
    exps = [jnp.exp(v - vals[0]) for v in vals]
    den = exps[0]
    for e in exps[1:]:
        den = den + e

    onehot = jnp.zeros((n_exp, tm), F32)
    for first in ids:
        onehot = onehot + (ei == first).astype(F32)
    before = _dot(onehot.astype(BF16), tri_ref[...]) + carry_ref[:, 0:1]
    wt_ref[TOP_K:SUBLANES, :] = jnp.zeros((SUBLANES - TOP_K, tm), F32)
    for k in range(TOP_K):
        idx_ref[k:k + 1, :] = ids[k]
        wt_ref[k:k + 1, :] = exps[k] / den
        rank = jnp.sum(jnp.where(ei == ids[k], before, 0.0), axis=0, keepdims=True)
        rank_ref[k:k + 1, :] = rank.astype(I32)
    carry_ref[...] = carry_ref[...] + jnp.sum(onehot, axis=1, keepdims=True)
    cnt_ref[...] = carry_ref[...].astype(I32)


def _output_projection(y_pool, y_nsa, w_out, x, gate1, norm_g, scale, shift, w_router_t, b_router, tri):
    b, s, d = x.shape
    tm = PROJ_TM
    n_exp = w_router_t.shape[0]
    t_all = b * s
    tok = lambda bi, m: (bi, m, 0)
    per_b = lambda bi, m: (bi, 0, 0)
    const2 = lambda bi, m: (0, 0)
    flat = lambda bi, m: (0, bi * (s // tm) + m)
    return pl.pallas_call(
        _outproj_kernel,
        grid=(b, s // tm),
        in_specs=[
            pl.BlockSpec((1, tm, POOL_WIDTH), tok),
            pl.BlockSpec((1, tm, NSA_WIDTH), tok),
            pl.BlockSpec((POOL_WIDTH + NSA_WIDTH, d), const2, pipeline_mode=pl.Buffered(1)),
            pl.BlockSpec((1, tm, d), tok),
            pl.BlockSpec((1, 1, d), per_b),
            pl.BlockSpec((1, d), const2),
            pl.BlockSpec((1, 1, d), per_b),
            pl.BlockSpec((1, 1, d), per_b),
            pl.BlockSpec((n_exp, d), const2),
            pl.BlockSpec((n_exp, 1), const2),
            pl.BlockSpec((tm, tm), const2, pipeline_mode=pl.Buffered(1)),
        ],
        out_specs=(
            pl.BlockSpec((1, tm, d), tok),
            pl.BlockSpec((1, tm, d // 2), tok),
            pl.BlockSpec((TOP_K, tm), flat),
            pl.BlockSpec((SUBLANES, tm), flat),
            pl.BlockSpec((TOP_K, tm), flat),
            pl.BlockSpec((n_exp, LANES), const2),
        ),
        out_shape=(
            jax.ShapeDtypeStruct((b, s, d), F32),
            jax.ShapeDtypeStruct((b, s, d // 2), jnp.uint32),
            jax.ShapeDtypeStruct((TOP_K, t_all), I32),
            jax.ShapeDtypeStruct((SUBLANES, t_all), F32),
            jax.ShapeDtypeStruct((TOP_K, t_all), I32),
            jax.ShapeDtypeStruct((n_exp, LANES), I32),
        ),
        scratch_shapes=[pltpu.VMEM((n_exp, LANES), F32)],
        compiler_params=_params("arbitrary", "arbitrary"),
        name="output_projection_router",
    )(y_pool, y_nsa, w_out, x, gate1, norm_g, scale, shift, w_router_t, b_router, tri)


def _dest_kernel(ps_ref, idx_ref, rank_ref, dest_ref):
    idx = idx_ref[...]
    dest = rank_ref[...]
    for e in range(ps_ref.shape[0]):
        dest = dest + jnp.where(idx == e, ps_ref[e], 0)
    dest_ref[...] = dest


def _destinations(seg_starts, idx_t, rank_t):
    return pl.pallas_call(
        _dest_kernel,
        in_specs=[
            pl.BlockSpec(memory_space=pltpu.SMEM),
            pl.BlockSpec(memory_space=pltpu.VMEM),
            pl.BlockSpec(memory_space=pltpu.VMEM),
        ],
        out_specs=pl.BlockSpec(memory_space=pltpu.VMEM),
        out_shape=jax.ShapeDtypeStruct(idx_t.shape, I32),
        name="moe_destinations",
    )(seg_starts, idx_t, rank_t)


def _row_copy(src_ref, src_row, dst_ref, dst_row, sem):
    return pltpu.make_async_copy(src_ref.at[pl.ds(src_row, 1)], dst_ref.at[pl.ds(dst_row, 1)], sem)


def _dispatch_kernel(dest_ref, starts_ref, ends_ref, h_ref, xs_ref, zero_ref, sem, zsem):
    tm = h_ref.shape[0]
    n_exp = starts_ref.shape[0]
    zrows = zero_ref.shape[0]

    @pl.when(pl.program_id(0) == 0)
    def _():
        zero_ref[...] = jnp.zeros(zero_ref.shape, zero_ref.dtype)

        def clear_block(end_row, start):
            for r in range(EXPERT_TM // zrows):
                row0 = pl.multiple_of(end_row - EXPERT_TM + r * zrows, zrows)
                cp = pltpu.make_async_copy(zero_ref, xs_ref.at[pl.ds(row0, zrows)], zsem)
                cp.start() if start else cp.wait()

        def clear(e, start):
            @pl.when(ends_ref[e] > starts_ref[e])
            def _():
                clear_block(ends_ref[e], start)

        first_unused = ends_ref[n_exp - 1] // EXPERT_TM
        n_blk = xs_ref.shape[0] // EXPERT_TM
        for start in (True, False):
            lax.fori_loop(0, n_exp, lambda e, c: (clear(e, start), c)[1], 0)
            lax.fori_loop(first_unused, n_blk, lambda i, c: (clear_block((i + 1) * EXPERT_TM, start), c)[1], 0)

    def issue(t, carry):
        for k in range(TOP_K):
            _row_copy(h_ref, t, xs_ref, dest_ref[k, t], sem).start()
        return carry

    lax.fori_loop(0, tm, issue, 0)

    def drain(t, carry):
        for k in range(TOP_K):
            _row_copy(h_ref, t, xs_ref, dest_ref[k, t], sem).wait()
        return carry

    lax.fori_loop(0, tm, drain, 0)


def _dispatch(dest_t, seg_starts, seg_ends, h_rows, n_buf):
    t_all, d = h_rows.shape
    tm = DISPATCH_TM
    return pl.pallas_call(
        _dispatch_kernel,
        grid=(t_all // tm,),
        in_specs=[
            pl.BlockSpec((TOP_K, tm), lambda i: (0, i), memory_space=pltpu.SMEM),
            pl.BlockSpec(memory_space=pltpu.SMEM),
            pl.BlockSpec(memory_space=pltpu.SMEM),
            pl.BlockSpec((tm, d), lambda i: (i, 0)),
        ],
        out_specs=pl.BlockSpec(memory_space=pl.ANY),
        out_shape=jax.ShapeDtypeStruct((n_buf, d), h_rows.dtype),
        scratch_shapes=[pltpu.VMEM((DISPATCH_ZERO_ROWS, d), h_rows.dtype),
                        pltpu.SemaphoreType.DMA(()), pltpu.SemaphoreType.DMA(())],
        compiler_params=_params("arbitrary"),
        name="moe_dispatch",
    )(dest_t, seg_starts, seg_ends, h_rows)


def _ffn_up_kernel(be_ref, nu_ref, xs_ref, wg_ref, bg_ref, wu_ref, bu_ref, h_ref, xb_ref):
    del be_ref
    i = pl.program_id(0)
    j = pl.program_id(1)
    used = i < nu_ref[0]

    @pl.when(used & (j == 0))
    def _():
        half = xs_ref.shape[1]
        first, second = _unpack_bf16_pair(xs_ref[...])
        xb_ref[:, 0:half] = first.astype(BF16)
        xb_ref[:, half:2 * half] = second.astype(BF16)

    @pl.when(used)
    def _():
        xb = xb_ref[...]
        gl = jnp.minimum(_dot(xb, wg_ref[0, 0].astype(BF16)) + bg_ref[0, 0], SWIGLU_LIMIT)
        lin = jnp.clip(_dot(xb, wu_ref[0, 0].astype(BF16)) + bu_ref[0, 0], -SWIGLU_LIMIT, SWIGLU_LIMIT)
        h_ref[...] = (gl * jax.nn.sigmoid(SWIGLU_ALPHA * gl) * (lin + 1.0)).astype(BF16)

    @pl.when(jnp.logical_not(used))
    def _():
        h_ref[...] = jnp.zeros(h_ref.shape, BF16)


def _ffn_down_kernel(be_ref, nu_ref, h_ref, wd_ref, bd_ref, y_ref):
    del be_ref
    used = pl.program_id(0) < nu_ref[0]

    @pl.when(used)
    def _():
        y = _dot(h_ref[...], wd_ref[0, 0].astype(BF16)) + bd_ref[0, 0]
        half = y.shape[1] // 2
        y_ref[...] = _pack_bf16_pair(y[:, 0:half], y[:, half:2 * half])

    @pl.when(jnp.logical_not(used))
    def _():
        y_ref[...] = jnp.zeros(y_ref.shape, jnp.uint32)


def _expert_ffn(layer, block_expert, n_used, xs, w_gate, b_gate, w_up, b_up, w_down, b_down):
    n_buf = xs.shape[0]
    depth, n_exp, d, f = w_gate.shape
    tm, tf = EXPERT_TM, EXPERT_TF
    nf = f // tf
    nd = d // tf
    n_blk = n_buf // tm

    def rows(i, j, be, nu):
        return (jnp.minimum(i, nu[0] - 1), 0)

    def cols(nj):
        return lambda i, j, be, nu: (layer, be[i], 0, jnp.where(i < nu[0], j, nj - 1))

    hidden = pl.pallas_call(
        _ffn_up_kernel,
        grid_spec=pltpu.PrefetchScalarGridSpec(
            num_scalar_prefetch=2,
            grid=(n_blk, nf),
            in_specs=[
                pl.BlockSpec((tm, d // 2), rows),
                pl.BlockSpec((1, 1, d, tf), cols(nf)),
                pl.BlockSpec((1, 1, 1, tf), cols(nf)),
                pl.BlockSpec((1, 1, d, tf), cols(nf)),
                pl.BlockSpec((1, 1, 1, tf), cols(nf)),
            ],
            out_specs=pl.BlockSpec((tm, tf), lambda i, j, be, nu: (i, j)),
            scratch_shapes=[pltpu.VMEM((tm, d), BF16)],
        ),
        out_shape=jax.ShapeDtypeStruct((n_buf, f), BF16),
        compiler_params=_params("arbitrary", "arbitrary"),
        name="moe_ffn_up",
    )(block_expert, n_used, xs, w_gate, b_gate.reshape(depth, n_exp, 1, f), w_up, b_up.reshape(depth, n_exp, 1, f))
    return pl.pallas_call(
        _ffn_down_kernel,
        grid_spec=pltpu.PrefetchScalarGridSpec(
            num_scalar_prefetch=2,
            grid=(n_blk, nd),
            in_specs=[
                pl.BlockSpec((tm, f), rows),
                pl.BlockSpec((1, 1, f, tf), cols(nd)),
                pl.BlockSpec((1, 1, 1, tf), cols(nd)),
            ],
            out_specs=pl.BlockSpec((tm, tf // 2), lambda i, j, be, nu: (i, j)),
        ),
        out_shape=jax.ShapeDtypeStruct((n_buf, d // 2), jnp.uint32),
        compiler_params=_params("arbitrary", "arbitrary"),
        name="moe_ffn_down",
    )(block_expert, n_used, hidden, w_down, b_down.reshape(depth, n_exp, 1, d))


def _combine_kernel(dest_ref, wt_ref, x1_ref, g2_ref, fg_ref, y_ref, o_ref, rows_ref, sem, *, final_norm):
    tm = x1_ref.shape[1]

    def issue(t, carry):
        for k in range(TOP_K):
            _row_copy(y_ref, dest_ref[k, t], rows_ref.at[k], t, sem).start()
        return carry

    lax.fori_loop(0, tm, issue, 0)

    wt = jnp.concatenate([wt_ref[...], jnp.zeros((tm - SUBLANES, tm), F32)], axis=0).T

    def drain(t, carry):
        for k in range(TOP_K):
            _row_copy(y_ref, dest_ref[k, t], rows_ref.at[k], t, sem).wait()
        return carry

    lax.fori_loop(0, tm, drain, 0)

    acc_first, acc_second = None, None
    for k in range(TOP_K):
        first, second = _unpack_bf16_pair(rows_ref[k])
        w = wt[:, k:k + 1]
        acc_first = w * first if k == 0 else acc_first + w * first
        acc_second = w * second if k == 0 else acc_second + w * second
    half = EXPERT_TF // 2
    pieces = []
    for j in range(acc_first.shape[1] // half):
        pieces += [acc_first[:, j * half:(j + 1) * half], acc_second[:, j * half:(j + 1) * half]]
    out = x1_ref[0] + g2_ref[0] * jnp.concatenate(pieces, axis=1)
    if final_norm:
        out = _rmsnorm(out, fg_ref[...])
    o_ref[0] = out


def _combine(dest_t, wt_t, x1, gate2, final_g, y_rows, final_norm):
    b, s, d = x1.shape
    tm = COMBINE_TM
    flat = lambda bi, m: (0, bi * (s // tm) + m)
    return pl.pallas_call(
        functools.partial(_combine_kernel, final_norm=final_norm),
        grid=(b, s // tm),
        in_specs=[
            pl.BlockSpec((TOP_K, tm), flat, memory_space=pltpu.SMEM),
            pl.BlockSpec((SUBLANES, tm), flat),
            pl.BlockSpec((1, tm, d), lambda bi, m: (bi, m, 0)),
            pl.BlockSpec((1, 1, d), lambda bi, m: (bi, 0, 0)),
            pl.BlockSpec((1, d), lambda bi, m: (0, 0)),
            pl.BlockSpec(memory_space=pl.ANY),
        ],
        out_specs=pl.BlockSpec((1, tm, d), lambda bi, m: (bi, m, 0)),
        out_shape=jax.ShapeDtypeStruct((b, s, d), F32),
        scratch_shapes=[pltpu.VMEM((TOP_K, tm, d // 2), jnp.uint32), pltpu.SemaphoreType.DMA(())],
        compiler_params=_params("arbitrary", "arbitrary"),
        name="moe_combine",
    )(dest_t, wt_t, x1, gate2, final_g, y_rows)


def _segment_layout(counts, n_blk):
    padded = (counts + EXPERT_TM - 1) // EXPERT_TM * EXPERT_TM
    ends = jnp.cumsum(padded)
    starts = ends - padded
    n_used = ends[-1] // EXPERT_TM
    blk = jnp.minimum(jnp.arange(n_blk, dtype=I32), n_used - 1) * EXPERT_TM
    block_expert = jnp.minimum(jnp.searchsorted(ends, blk, side="right"), counts.shape[0] - 1)
    return starts.astype(I32), ends.astype(I32), block_expert.astype(I32), n_used.reshape(1).astype(I32)


def kernel(x, c, norm1_g, norm2_g, w_ada, b_ada, w_in, pool_w, pool_scale, cmp_pos_k, cmp_w1_k, cmp_b1_k, cmp_w2_k, cmp_b2_k, cmp_pos_v, cmp_w1_v, cmp_b1_v, cmp_w2_v, cmp_b2_v, w_out, w_router, b_router, w_gate, b_gate, w_up, b_up, w_down, b_down, final_g):
    b, s, d = x.shape
    depth = w_ada.shape[0]
    n_exp = w_router.shape[-1]
    t_all = b * s
    assert s % PROJ_TM == 0 and s % SLC_CHUNK == 0 and t_all % DISPATCH_TM == 0
    n_main = POOL_WIDTH + NSA_WIDTH + 6 * KV_WIDTH
    assert w_in.shape[-1] == n_main + N_GATE_COLS

    c_pad = jnp.pad(c, ((0, SUBLANES - b), (0, 0)))
    mod = _modulation(c_pad, w_ada, b_ada)[:, :b]
    tri = (jnp.arange(PROJ_TM)[:, None] < jnp.arange(PROJ_TM)[None, :]).astype(BF16)
    n_buf = (t_all * TOP_K + n_exp * (EXPERT_TM - 1)) // EXPERT_TM * EXPERT_TM
    n_blk = n_buf // EXPERT_TM

    for l in range(depth):
        sh1, sc1, g1, sh2, sc2, g2 = [m.reshape(b, 1, d) for m in jnp.split(mod[l], 6, axis=-1)]
        w_main = w_in[l, :, :n_main].astype(BF16)
        w_gates = jnp.pad(w_in[l, :, n_main:], ((0, 0), (0, LANES - N_GATE_COLS))).astype(BF16)
        u_pool, qT, kvc, ksl, vslT, kwin, vwinT, gT = _input_projection(
            x, norm1_g[l].reshape(1, d), sc1, sh1, w_main, w_gates)
        y_pool = _pool_mixer(u_pool, pool_w[l].astype(BF16), pool_scale[l].reshape(1, POOL_WIDTH))
        kc, kcT = _compress(
            kvc,
            jnp.stack([cmp_pos_k[l], cmp_pos_v[l]]).reshape(2, 1, CMP_BLOCK * HEAD_DIM),
            jnp.stack([cmp_w1_k[l], cmp_w1_v[l]]).astype(BF16),
            jnp.stack([cmp_b1_k[l], cmp_b1_v[l]])[:, None, :],
            jnp.stack([cmp_w2_k[l], cmp_w2_v[l]]).astype(BF16),
            jnp.stack([cmp_b2_k[l], cmp_b2_v[l]])[:, None, :])
        y_nsa = _nsa(qT, kc, kcT, ksl, vslT, kwin, vwinT, gT)
        x1, h2, idx_t, wt_t, rank_t, counts = _output_projection(
            y_pool, y_nsa, w_out[l].astype(BF16), x, g1, norm2_g[l].reshape(1, d), sc2, sh2,
            w_router[l].T.astype(BF16), b_router[l].reshape(n_exp, 1), tri)
        seg_starts, seg_ends, block_expert, n_used = _segment_layout(counts[:, 0], n_blk)
        dest_t = _destinations(seg_starts, idx_t, rank_t)
        xs = _dispatch(dest_t, seg_starts, seg_ends, h2.reshape(t_all, d // 2), n_buf)
        y_rows = _expert_ffn(l, block_expert, n_used, xs, w_gate, b_gate, w_up, b_up, w_down, b_down)
        x = _combine(dest_t, wt_t, x1, g2, final_g.reshape(1, d), y_rows, final_norm=(l == depth - 1))
    return x
```

```python
import functools

import jax
import jax.numpy as jnp
from jax import lax
from jax.experimental import pallas as pl
from jax.experimental.pallas import tpu as pltpu

F32 = jnp.float32
BF16 = jnp.bfloat16
I32 = jnp.int32

HEAD_DIM = 128
N_HEADS = 8
N_KV_GROUPS = 2
HEADS_PER_GROUP = N_HEADS // N_KV_GROUPS
POOL_WINDOWS = (2, 4, 8, 16)
POOL_GROUP = 256
POOL_WIDTH = POOL_GROUP * len(POOL_WINDOWS)
NSA_WIDTH = N_HEADS * HEAD_DIM
KV_WIDTH = N_KV_GROUPS * HEAD_DIM
N_GATE_COLS = 3 * N_HEADS
CMP_BLOCK = 32
CMP_STRIDE = 16
SEL_BLOCK = 64
SEL_RATIO = SEL_BLOCK // CMP_STRIDE
N_SELECT = 16
WINDOW = 512
Q_BLOCK = 128
TOP_K = 4
SWIGLU_ALPHA = 1.702
SWIGLU_LIMIT = 7.0
NORM_EPS = 1e-5
NEG = -1e30
BIG = 1e30
BELOW_NEG = -3e38
V_ROWS = HEAD_DIM + 16
LOG2_E = 1.4426950408889634

LANES = 128
SUBLANES = 8
VMEM_LIMIT_BYTES = 56 * 1024 * 1024

MOD_TN = 1024
PROJ_TM = 512
SLC_CHUNK = 256
SLC_UNROLL = 6
WIN_CHUNK = 128
DISPATCH_TM = 256
DISPATCH_ZERO_ROWS = 256
COMBINE_TM = 128
EXPERT_TM = 512
EXPERT_TF = 512
EXPERT_TD = 1024


def _params(*sem):
    return pltpu.CompilerParams(dimension_semantics=sem, vmem_limit_bytes=VMEM_LIMIT_BYTES)


def _dot(a, b):
    return jnp.dot(a, b, preferred_element_type=F32)


def _rmsnorm(x, g):
    return x * lax.rsqrt(jnp.mean(x * x, axis=-1, keepdims=True) + NORM_EPS) * g


_HIGH_HALF = 0xFFFF0000


def _pack_bf16_pair(a, b):
    hi = pltpu.bitcast(a.astype(BF16).astype(F32), jnp.uint32) & jnp.uint32(_HIGH_HALF)
    lo = pltpu.bitcast(b.astype(BF16).astype(F32), jnp.uint32) >> 16
    return hi | lo


def _unpack_bf16_pair(w):
    return pltpu.bitcast(w & jnp.uint32(_HIGH_HALF), F32), pltpu.bitcast(w << 16, F32)


def _mod_kernel(c_ref, w_ref, b_ref, o_ref):
    c = c_ref[...]
    ca = (c * jax.nn.sigmoid(c)).astype(BF16)
    o_ref[0] = _dot(ca, w_ref[0].astype(BF16)) + b_ref[0]


def _modulation(c_pad, w_ada, b_ada):
    depth, d, n = w_ada.shape
    rows = c_pad.shape[0]
    return pl.pallas_call(
        _mod_kernel,
        grid=(depth, n // MOD_TN),
        in_specs=[
            pl.BlockSpec((rows, d), lambda l, j: (0, 0)),
            pl.BlockSpec((1, d, MOD_TN), lambda l, j: (l, 0, j)),
            pl.BlockSpec((1, 1, MOD_TN), lambda l, j: (l, 0, j)),
        ],
        out_specs=pl.BlockSpec((1, rows, MOD_TN), lambda l, j: (l, 0, j)),
        out_shape=jax.ShapeDtypeStruct((depth, rows, n), F32),
        compiler_params=_params("arbitrary", "arbitrary"),
        name="adaln_modulation",
    )(c_pad, w_ada, b_ada.reshape(depth, 1, n))


def _inproj_kernel(x_ref, g_ref, sc_ref, sh_ref, w_ref, wg_ref,
                   upool_ref, qT_ref, kvc_ref, ksl_ref, vslT_ref, kwin_ref, vwinT_ref, gT_ref):
    tm = x_ref.shape[1]
    h = _rmsnorm(x_ref[0], g_ref[...]) * (1.0 + sc_ref[0]) + sh_ref[0]
    hb = h.astype(BF16)
    upool_ref[0] = _dot(hb, w_ref[:, 0:POOL_WIDTH])
    q0 = POOL_WIDTH
    scale = HEAD_DIM ** -0.5 * LOG2_E
    def four_heads(col0):
        wide = _dot(hb, w_ref[:, col0:col0 + 4 * HEAD_DIM])
        return [wide[:, i * HEAD_DIM:(i + 1) * HEAD_DIM] for i in range(4)]

    for j in range(N_HEADS // 4):
        for i, q in enumerate(four_heads(q0 + j * 4 * HEAD_DIM)):
            hd = 4 * j + i
            qT_ref[0, hd * HEAD_DIM:(hd + 1) * HEAD_DIM, :] = (q * scale).T.astype(BF16)
    c0 = q0 + NSA_WIDTH
    for i, kv in enumerate(four_heads(c0)):
        kvc_ref[0, i] = kv
    s0 = c0 + 2 * KV_WIDTH
    w0 = s0 + 2 * KV_WIDTH
    slc = four_heads(s0)
    win = four_heads(w0)
    tpos = pl.program_id(1) * tm + lax.broadcasted_iota(I32, (tm, HEAD_DIM), 0)
    lane = lax.broadcasted_iota(I32, (tm, HEAD_DIM), 1)
    block_onehot = jnp.where(lane == ((tpos // SEL_BLOCK) & (LANES - 1)), 1.0, 0.0).astype(BF16)
    for g in range(N_KV_GROUPS):
        ksl_ref[0, g, :, 0:HEAD_DIM] = slc[g].astype(BF16)
        ksl_ref[0, g, :, HEAD_DIM:2 * HEAD_DIM] = block_onehot
        vT = slc[N_KV_GROUPS + g].T.astype(BF16)
        for c in range(tm // SLC_CHUNK):
            vslT_ref[0, g, c, 0:HEAD_DIM, :] = vT[:, c * SLC_CHUNK:(c + 1) * SLC_CHUNK]
            vslT_ref[0, g, c, HEAD_DIM:V_ROWS, :] = jnp.ones((V_ROWS - HEAD_DIM, SLC_CHUNK), BF16)
        kwin_ref[0, g] = win[g].astype(BF16)
        vT = win[N_KV_GROUPS + g].T.astype(BF16)
        for c in range(tm // WIN_CHUNK):
            vwinT_ref[0, g, c, 0:HEAD_DIM, :] = vT[:, c * WIN_CHUNK:(c + 1) * WIN_CHUNK]
            vwinT_ref[0, g, c, HEAD_DIM:V_ROWS, :] = jnp.ones((V_ROWS - HEAD_DIM, WIN_CHUNK), BF16)
    gl = jax.nn.sigmoid(_dot(hb, wg_ref[...]))
    gT_ref[0] = gl.T[0:N_GATE_COLS, :]


def _input_projection(x, norm_g, scale, shift, w_main, w_gates):
    b, s, d = x.shape
    tm = PROJ_TM
    n_main = w_main.shape[1]
    tok = lambda bi, m: (bi, m, 0)
    per_b = lambda bi, m: (bi, 0, 0)
    const2 = lambda bi, m: (0, 0)
    out_shape = (
        jax.ShapeDtypeStruct((b, s, POOL_WIDTH), F32),
        jax.ShapeDtypeStruct((b, NSA_WIDTH, s), BF16),
        jax.ShapeDtypeStruct((b, 2 * N_KV_GROUPS, s, HEAD_DIM), F32),
        jax.ShapeDtypeStruct((b, N_KV_GROUPS, s, 2 * HEAD_DIM), BF16),
        jax.ShapeDtypeStruct((b, N_KV_GROUPS, s // SLC_CHUNK, V_ROWS, SLC_CHUNK), BF16),
        jax.ShapeDtypeStruct((b, N_KV_GROUPS, s, HEAD_DIM), BF16),
        jax.ShapeDtypeStruct((b, N_KV_GROUPS, s // WIN_CHUNK, V_ROWS, WIN_CHUNK), BF16),
        jax.ShapeDtypeStruct((b, N_GATE_COLS, s), F32),
    )
    out_specs = (
        pl.BlockSpec((1, tm, POOL_WIDTH), tok),
        pl.BlockSpec((1, NSA_WIDTH, tm), lambda bi, m: (bi, 0, m)),
        pl.BlockSpec((1, 2 * N_KV_GROUPS, tm, HEAD_DIM), lambda bi, m: (bi, 0, m, 0)),
        pl.BlockSpec((1, N_KV_GROUPS, tm, 2 * HEAD_DIM), lambda bi, m: (bi, 0, m, 0)),
        pl.BlockSpec((1, N_KV_GROUPS, tm // SLC_CHUNK, V_ROWS, SLC_CHUNK), lambda bi, m: (bi, 0, m, 0, 0)),
        pl.BlockSpec((1, N_KV_GROUPS, tm, HEAD_DIM), lambda bi, m: (bi, 0, m, 0)),
        pl.BlockSpec((1, N_KV_GROUPS, tm // WIN_CHUNK, V_ROWS, WIN_CHUNK), lambda bi, m: (bi, 0, m, 0, 0)),
        pl.BlockSpec((1, N_GATE_COLS, tm), lambda bi, m: (bi, 0, m)),
    )
    return pl.pallas_call(
        _inproj_kernel,
        grid=(b, s // tm),
        in_specs=[
            pl.BlockSpec((1, tm, d), tok),
            pl.BlockSpec((1, d), const2),
            pl.BlockSpec((1, 1, d), per_b),
            pl.BlockSpec((1, 1, d), per_b),
            pl.BlockSpec((d, n_main), const2, pipeline_mode=pl.Buffered(1)),
            pl.BlockSpec((d, LANES), const2, pipeline_mode=pl.Buffered(1)),
        ],
        out_specs=out_specs,
        out_shape=out_shape,
        compiler_params=_params("arbitrary", "arbitrary"),
        name="input_projection",
    )(x, norm_g, scale, shift, w_main, w_gates)


def _pool_kernel(u_ref, pw_ref, ps_ref, o_ref, ext_ref):
    tm = u_ref.shape[1]
    halo = max(POOL_WINDOWS)
    m = pl.program_id(1)

    @pl.when(m == 0)
    def _():
        ext_ref[0:halo, :] = jnp.zeros((halo, POOL_WIDTH), F32)

    ext_ref[halo:halo + tm, :] = u_ref[0]
    t = m * tm + lax.broadcasted_iota(I32, (tm, 1), 0)
    for gi, w in enumerate(POOL_WINDOWS):
        cols = slice(gi * POOL_GROUP, (gi + 1) * POOL_GROUP)
        ug = ext_ref[halo:halo + tm, cols]
        acc = ug
        for i in range(1, w):
            acc = acc + ext_ref[halo - i:halo - i + tm, cols]
        cnt = jnp.minimum(t + 1, w).astype(F32)
        diff = (acc / cnt - ug).astype(BF16)
        o_ref[0, :, cols] = (_dot(diff, pw_ref[gi]) * ps_ref[:, cols]).astype(BF16)
    ext_ref[0:halo, :] = ext_ref[tm:tm + halo, :]


def _pool_mixer(u_pool, pool_w, pool_scale):
    b, s, _ = u_pool.shape
    tm = PROJ_TM
    return pl.pallas_call(
        _pool_kernel,
        grid=(b, s // tm),
        in_specs=[
            pl.BlockSpec((1, tm, POOL_WIDTH), lambda bi, m: (bi, m, 0)),
            pl.BlockSpec((len(POOL_WINDOWS), POOL_GROUP, POOL_GROUP), lambda bi, m: (0, 0, 0)),
            pl.BlockSpec((1, POOL_WIDTH), lambda bi, m: (0, 0)),
        ],
        out_specs=pl.BlockSpec((1, tm, POOL_WIDTH), lambda bi, m: (bi, m, 0)),
        out_shape=jax.ShapeDtypeStruct((b, s, POOL_WIDTH), BF16),
        scratch_shapes=[pltpu.VMEM((tm + max(POOL_WINDOWS), POOL_WIDTH), F32)],
        compiler_params=_params("arbitrary", "arbitrary"),
        name="pool_mixer",
    )(u_pool, pool_w, pool_scale)


def _compress_kernel(a_ref, pos_ref, w1_ref, b1_ref, w2_ref, b2_ref, o_ref, oT_ref):
    half = CMP_STRIDE * HEAD_DIM
    a = a_ref[0, 0]
    nb = a.shape[0]
    pos = pos_ref[0]
    a_lo = (a + pos[:, 0:half]).astype(BF16)
    a_hi = (a + pos[:, half:2 * half]).astype(BF16)
    p_lo = _dot(a_lo, w1_ref[0, 0:half, :])
    p_hi = _dot(a_hi, w1_ref[0, half:2 * half, :])
    hid = jax.nn.gelu(p_lo + pltpu.roll(p_hi, nb - 1, 0) + b1_ref[0])
    out = _dot(hid.astype(BF16), w2_ref[0]) + b2_ref[0]
    o_ref[0, 0] = out.astype(BF16)
    oT_ref[0, 0] = out.T.astype(BF16)


def _compress(kvc, pos, w1, b1, w2, b2):
    b, n4, s, _ = kvc.shape
    nb = s // CMP_STRIDE
    a = kvc.reshape(b, n4, nb, CMP_STRIDE * HEAD_DIM)
    hidden = w1.shape[-1]
    kv = lambda bi, i: (i // N_KV_GROUPS, 0, 0)
    return pl.pallas_call(
        _compress_kernel,
        grid=(b, n4),
        in_specs=[
            pl.BlockSpec((1, 1, nb, CMP_STRIDE * HEAD_DIM), lambda bi, i: (bi, i, 0, 0)),
            pl.BlockSpec((1, 1, CMP_BLOCK * HEAD_DIM), kv),
            pl.BlockSpec((1, CMP_BLOCK * HEAD_DIM, hidden), kv),
            pl.BlockSpec((1, 1, hidden), kv),
            pl.BlockSpec((1, hidden, HEAD_DIM), kv),
            pl.BlockSpec((1, 1, HEAD_DIM), kv),
        ],
        out_specs=(
            pl.BlockSpec((1, 1, nb, HEAD_DIM), lambda bi, i: (bi, i, 0, 0)),
            pl.BlockSpec((1, 1, HEAD_DIM, nb), lambda bi, i: (bi, i, 0, 0)),
        ),
        out_shape=(
            jax.ShapeDtypeStruct((b, n4, nb, HEAD_DIM), BF16),
            jax.ShapeDtypeStruct((b, n4, HEAD_DIM, nb), BF16),
        ),
        compiler_params=_params("arbitrary", "arbitrary"),
        name="kv_compress",
    )(a, pos, w1, b1, w2, b2)


def _flash_step(s, v_t, m_old, acc_ref):
    m_new = jnp.maximum(m_old, jnp.max(s, axis=0, keepdims=True))
    alpha = jnp.exp2(m_old - m_new)
    p = jnp.exp2(s - m_new)
    acc_ref[...] = acc_ref[...] * alpha + _dot(v_t, p.astype(BF16))
    return m_new


def _nsa_kernel(qT_ref, kc_ref, vcT_ref, ksl_ref, vslT_ref, kwin_ref, vwinT_ref, gT_ref, y_ref,
                qcat_ref, pbuf_ref, score_ref, sel_ref, wq_ref, sa_ref, sb_ref, sc_ref, ml_ref, wbuf_ref,
                ocmp_ref, aslc_ref, awin_ref):
    n = pl.program_id(2)
    q_start = n * Q_BLOCK
    nq = HEADS_PER_GROUP * Q_BLOCK
    nb = kc_ref.shape[2]
    ns = sel_ref.shape[0]

    for r in range(HEADS_PER_GROUP):
        qcat_ref[:, r * Q_BLOCK:(r + 1) * Q_BLOCK] = qT_ref[0, r * HEAD_DIM:(r + 1) * HEAD_DIM, :]
    qc = qcat_ref[...]
    tin = lax.broadcasted_iota(I32, (1, nq), 1) & (Q_BLOCK - 1)
    tok = q_start + tin

    sel_ref[...] = jnp.full((ns, Q_BLOCK), NEG, F32)
    t1 = q_start + lax.broadcasted_iota(I32, (1, Q_BLOCK), 1)
    tj = t1 // SEL_BLOCK

    def compress_and_select(rows):
        srows = rows // SEL_RATIO
        s = _dot(kc_ref[0, 0, 0:rows, :], qc)
        ci = lax.broadcasted_iota(I32, (rows, nq), 0)
        s = jnp.where(ci * CMP_STRIDE + (CMP_BLOCK - 1) <= tok, s, NEG)
        m = jnp.max(s, axis=0, keepdims=True)
        e = jnp.exp2(s - m)
        l = jnp.sum(e, axis=0, keepdims=True)
        inv = jnp.where(tok >= CMP_BLOCK - 1, 1.0 / l, 0.0)
        p = e * inv
        ocmp_ref[...] = _dot(vcT_ref[0, 0, :, 0:rows], p.astype(BF16))

        psum = p[:, 0:Q_BLOCK]
        for r in range(1, HEADS_PER_GROUP):
            psum = psum + p[:, r * Q_BLOCK:(r + 1) * Q_BLOCK]
        pbuf_ref[0:SUBLANES, :] = jnp.zeros((SUBLANES, Q_BLOCK), F32)
        pbuf_ref[SUBLANES:SUBLANES + rows, :] = psum
        imp = pbuf_ref[pl.ds(SUBLANES - 1, srows, stride=SEL_RATIO), :]
        for d in range(SEL_RATIO):
            imp = imp + pbuf_ref[pl.ds(SUBLANES + d, srows, stride=SEL_RATIO), :]

        ji = lax.broadcasted_iota(I32, (srows, Q_BLOCK), 0)
        forced = (ji == 0) | (ji == tj) | (ji == tj - 1)
        sel_ref[0:srows, :] = jnp.where(forced, 0.0, NEG)
        score_ref[0:srows, :] = jnp.where(forced, BELOW_NEG, jnp.where(ji * SEL_BLOCK <= t1, imp, NEG))
        n_forced = 1 + (tj > 0).astype(I32) + (tj > 1).astype(I32)
        quota = min(N_SELECT, srows) - n_forced

        def pick(it, carry):
            sc = score_ref[0:srows, :]
            mx = jnp.max(sc, axis=0, keepdims=True)
            first = jnp.min(jnp.where(sc == mx, ji, srows), axis=0, keepdims=True)
            hit = ji == first
            sel_ref[0:srows, :] = jnp.where(hit & (mx >= 0.0) & (it < quota), 0.0, sel_ref[0:srows, :])
            score_ref[0:srows, :] = jnp.where(hit, BELOW_NEG, sc)
            return carry

        n_picks = min(N_SELECT, srows) - jnp.where(n == 0, 1, 3)
        lax.fori_loop(0, n_picks, pick, 0)

    per_q = Q_BLOCK // CMP_STRIDE
    classes = sorted({r for r in (nb // 4, nb // 2) if r % (SEL_RATIO * SUBLANES * 2) == 0} | {nb})
    for k, rows in enumerate(classes):
        lo = 0 if k == 0 else classes[k - 1] // per_q
        in_class = (n >= lo) if k == len(classes) - 1 else ((n >= lo) & (n < rows // per_q))
        pl.when(in_class)(functools.partial(compress_and_select, rows))

    sel = sel_ref[...]
    wq_ref[wq_ref.shape[0] - 1, 0:HEAD_DIM, :] = jnp.zeros((HEAD_DIM, nq), BF16)
    wq_ref[wq_ref.shape[0] - 1, HEAD_DIM:HEAD_DIM + LANES, :] = jnp.full((LANES, nq), NEG, BF16)
    for h in range(wq_ref.shape[0] - 1):
        rows_h = min(LANES, ns - LANES * h)
        wq_ref[h, 0:HEAD_DIM, :] = qc
        blk = sel[LANES * h:LANES * h + rows_h].astype(BF16)
        for r in range(HEADS_PER_GROUP):
            wq_ref[h, HEAD_DIM:HEAD_DIM + rows_h, r * Q_BLOCK:(r + 1) * Q_BLOCK] = blk
        if rows_h < LANES:
            wq_ref[h, HEAD_DIM + rows_h:HEAD_DIM + LANES, :] = jnp.zeros((LANES - rows_h, nq), BF16)

    m0 = jnp.full((1, nq), NEG, F32)

    aslc_ref[...] = jnp.zeros(aslc_ref.shape, F32)
    blocks_per_chunk = SLC_CHUNK // SEL_BLOCK

    ml_ref[0:1, :] = m0

    n_past = q_start // SLC_CHUNK
    last_chunk = ksl_ref.shape[2] // SLC_CHUNK - 1
    n_half = wq_ref.shape[0] - 1

    def slc_scores(c, buf_ref):
        slot = jnp.where(c < n_past, (c * blocks_per_chunk) // LANES, n_half)
        k0 = pl.multiple_of(jnp.minimum(c, last_chunk) * SLC_CHUNK, SLC_CHUNK)
        buf_ref[...] = _dot(ksl_ref[0, 0, pl.ds(k0, SLC_CHUNK), :], wq_ref[slot])

    def slc_flash(c, buf_ref):
        v_t = vslT_ref[0, 0, jnp.minimum(c, last_chunk)]
        ml_ref[0:1, :] = _flash_step(buf_ref[...], v_t, ml_ref[0:1, :], aslc_ref)

    def slc_round(i, carry):
        bufs = (sa_ref, sb_ref, sc_ref)
        for k in range(SLC_UNROLL):
            slc_scores(SLC_UNROLL * i + k + 2, bufs[(k + 2) % 3])
            slc_flash(SLC_UNROLL * i + k, bufs[k % 3])
        return carry

    slc_scores(0, sa_ref)
    slc_scores(1, sb_ref)
    lax.fori_loop(0, (n_past + SLC_UNROLL - 1) // SLC_UNROLL, slc_round, 0)

    k0 = pl.multiple_of(n_past * SLC_CHUNK, SLC_CHUNK)
    sd = _dot(ksl_ref[0, 0, pl.ds(k0, SLC_CHUNK), :], wq_ref[(n_past * blocks_per_chunk) // LANES])
    kpos = k0 + lax.broadcasted_iota(I32, (SLC_CHUNK, nq), 0)
    sd = jnp.where(kpos <= tok, sd, NEG)
    _flash_step(sd, vslT_ref[0, 0, n_past], ml_ref[0:1, :], aslc_ref)

    n_wc = WINDOW // WIN_CHUNK + 1
    kk = lax.broadcasted_iota(I32, (WIN_CHUNK, nq), 0)
    chunk_ids = []
    m_win = m0
    for i in range(n_wc):
        wb = n - (n_wc - 1) + i
        wbc = jnp.maximum(wb, 0)
        chunk_ids.append(wbc)
        k0 = pl.multiple_of(wbc * WIN_CHUNK, WIN_CHUNK)
        sc = _dot(kwin_ref[0, 0, pl.ds(k0, WIN_CHUNK), :], qc)
        if i == 0:
            sc = jnp.where(kk > tin, sc, NEG)
        if i == n_wc - 1:
            sc = jnp.where(kk <= tin, sc, NEG)
        else:
            sc = sc + jnp.where(wb >= 0, 0.0, NEG)
        wbuf_ref[i] = sc
        m_win = jnp.maximum(m_win, jnp.max(sc, axis=0, keepdims=True))
    for i in range(n_wc):
        p = jnp.exp2(wbuf_ref[i] - m_win)
        pv = _dot(vwinT_ref[0, 0, chunk_ids[i]], p.astype(BF16))
        if i == 0:
            awin_ref[...] = pv
        else:
            awin_ref[...] = awin_ref[...] + pv

    gates = gT_ref[0, 0]
    inv_slc = 1.0 / aslc_ref[HEAD_DIM:HEAD_DIM + 1, :]
    inv_win = 1.0 / awin_ref[HEAD_DIM:HEAD_DIM + 1, :]
    for r in range(HEADS_PER_GROUP):
        cols = slice(r * Q_BLOCK, (r + 1) * Q_BLOCK)
        o = (gates[3 * r:3 * r + 1] * ocmp_ref[:, cols]
             + gates[3 * r + 1:3 * r + 2] * inv_slc[:, cols] * aslc_ref[0:HEAD_DIM, cols]
             + gates[3 * r + 2:3 * r + 3] * inv_win[:, cols] * awin_ref[0:HEAD_DIM, cols])
        y_ref[0, :, r * HEAD_DIM:(r + 1) * HEAD_DIM] = o.T.astype(BF16)


def _nsa(qT, kc, kcT, ksl, vslT, kwin, vwinT, gT):
    b, _, s = qT.shape
    nb = kc.shape[2]
    ns = s // SEL_BLOCK
    g4 = HEADS_PER_GROUP * HEAD_DIM
    nq = HEADS_PER_GROUP * Q_BLOCK
    gates = gT.reshape(b, N_KV_GROUPS, HEADS_PER_GROUP * 3, s)
    per_bg = lambda bi, g, n: (bi, g, 0, 0)
    per_bg5 = lambda bi, g, n: (bi, g, 0, 0, 0)
    once = pl.Buffered(1)
    return pl.pallas_call(
        _nsa_kernel,
        grid=(b, N_KV_GROUPS, s // Q_BLOCK),
        in_specs=[
            pl.BlockSpec((1, g4, Q_BLOCK), lambda bi, g, n: (bi, g, n)),
            pl.BlockSpec((1, 1, nb, HEAD_DIM), per_bg, pipeline_mode=once),
            pl.BlockSpec((1, 1, HEAD_DIM, nb), lambda bi, g, n: (bi, N_KV_GROUPS + g, 0, 0), pipeline_mode=once),
            pl.BlockSpec((1, 1, s, 2 * HEAD_DIM), per_bg, pipeline_mode=once),
            pl.BlockSpec((1, 1, s // SLC_CHUNK, V_ROWS, SLC_CHUNK), per_bg5, pipeline_mode=once),
            pl.BlockSpec((1, 1, s, HEAD_DIM), per_bg, pipeline_mode=once),
            pl.BlockSpec((1, 1, s // WIN_CHUNK, V_ROWS, WIN_CHUNK), per_bg5, pipeline_mode=once),
            pl.BlockSpec((1, 1, HEADS_PER_GROUP * 3, Q_BLOCK), lambda bi, g, n: (bi, g, 0, n)),
        ],
        out_specs=pl.BlockSpec((1, Q_BLOCK, g4), lambda bi, g, n: (bi, n, g)),
        out_shape=jax.ShapeDtypeStruct((b, s, NSA_WIDTH), BF16),
        scratch_shapes=[
            pltpu.VMEM((HEAD_DIM, nq), BF16),
            pltpu.VMEM((nb + SUBLANES, Q_BLOCK), F32),
            pltpu.VMEM((ns, Q_BLOCK), F32),
            pltpu.VMEM((ns, Q_BLOCK), F32),
            pltpu.VMEM((pl.cdiv(ns, LANES) + 1, 2 * HEAD_DIM, nq), BF16),
            pltpu.VMEM((SLC_CHUNK, nq), F32),
            pltpu.VMEM((SLC_CHUNK, nq), F32),
            pltpu.VMEM((SLC_CHUNK, nq), F32),
            pltpu.VMEM((SUBLANES, nq), F32),
            pltpu.VMEM((WINDOW // WIN_CHUNK + 1, WIN_CHUNK, nq), F32),
            pltpu.VMEM((HEAD_DIM, nq), F32),
            pltpu.VMEM((V_ROWS, nq), F32),
            pltpu.VMEM((V_ROWS, nq), F32),
        ],
        compiler_params=_params("arbitrary", "arbitrary", "arbitrary"),
        name="nsa_attention",
    )(qT, kc, kcT, ksl, vslT, kwin, vwinT, gates)


def _outproj_kernel(yp_ref, yn_ref, wo_ref, x_ref, g1_ref, n2_ref, sc_ref, sh_ref, wrT_ref, br_ref, tri_ref,
                    x1_ref, h2_ref, idx_ref, wt_ref, rank_ref, cnt_ref, carry_ref):
    tm = x_ref.shape[1]
    n_exp = wrT_ref.shape[0]

    @pl.when((pl.program_id(0) == 0) & (pl.program_id(1) == 0))
    def _():
        carry_ref[...] = jnp.zeros(carry_ref.shape, F32)

    mix = _dot(yp_ref[0], wo_ref[0:POOL_WIDTH, :]) + _dot(yn_ref[0], wo_ref[POOL_WIDTH:POOL_WIDTH + NSA_WIDTH, :])
    x1 = x_ref[0] + g1_ref[0] * mix
    x1_ref[0] = x1
    h2 = _rmsnorm(x1, n2_ref[...]) * (1.0 + sc_ref[0]) + sh_ref[0]
    half = h2.shape[1] // 2
    h2_ref[0] = _pack_bf16_pair(h2[:, 0:half], h2[:, half:2 * half])

    logits = lax.dot_general(wrT_ref[...], h2.astype(BF16), (((1,), (1,)), ((), ())),
                             preferred_element_type=F32) + br_ref[...]
    ei = lax.broadcasted_iota(I32, (n_exp, tm), 0)
    vals, ids = [], []
    for _ in range(TOP_K):
        mx = jnp.max(logits, axis=0, keepdims=True)
        first = jnp.min(jnp.where(logits == mx, ei, n_exp), axis=0, keepdims=True)
        vals.append(mx)
        ids.append(first)
        logits = jnp.where(ei == first, -jnp.inf, logits)
    exps = [jnp.exp(v - vals[0]) for v in vals]
    den = exps[0]
    for e in exps[1:]:
        den = den + e

    onehot = jnp.zeros((n_exp, tm), F32)
    for first in ids:
        onehot = onehot + (ei == first).astype(F32)
    before = _dot(onehot.astype(BF16), tri_ref[...]) + carry_ref[:, 0:1]
    wt_ref[TOP_K:SUBLANES, :] = jnp.zeros((SUBLANES - TOP_K, tm), F32)
    for k in range(TOP_K):
        idx_ref[k:k + 1, :] = ids[k]
        wt_ref[k:k + 1, :] = exps[k] / den
        rank = jnp.sum(jnp.where(ei == ids[k], before, 0.0), axis=0, keepdims=True)
        rank_ref[k:k + 1, :] = rank.astype(I32)
    carry_ref[...] = carry_ref[...] + jnp.sum(onehot, axis=1, keepdims=True)
    cnt_ref[...] = carry_ref[...].astype(I32)


def _output_projection(y_pool, y_nsa, w_out, x, gate1, norm_g, scale, shift, w_router_t, b_router, tri):
    b, s, d = x.shape
    tm = PROJ_TM
    n_exp = w_router_t.shape[0]
    t_all = b * s
    tok = lambda bi, m: (bi, m, 0)
    per_b = lambda bi, m: (bi, 0, 0)
    const2 = lambda bi, m: (0, 0)
    flat = lambda bi, m: (0, bi * (s // tm) + m)
    return pl.pallas_call(
        _outproj_kernel,
        grid=(b, s // tm),
        in_specs=[
            pl.BlockSpec((1, tm, POOL_WIDTH), tok),
            pl.BlockSpec((1, tm, NSA_WIDTH), tok),
            pl.BlockSpec((POOL_WIDTH + NSA_WIDTH, d), const2, pipeline_mode=pl.Buffered(1)),
            pl.BlockSpec((1, tm, d), tok),
            pl.BlockSpec((1, 1, d), per_b),
            pl.BlockSpec((1, d), const2),
            pl.BlockSpec((1, 1, d), per_b),
            pl.BlockSpec((1, 1, d), per_b),
            pl.BlockSpec((n_exp, d), const2),
            pl.BlockSpec((n_exp, 1), const2),
            pl.BlockSpec((tm, tm), const2, pipeline_mode=pl.Buffered(1)),
        ],
        out_specs=(
            pl.BlockSpec((1, tm, d), tok),
            pl.BlockSpec((1, tm, d // 2), tok),
            pl.BlockSpec((TOP_K, tm), flat),
            pl.BlockSpec((SUBLANES, tm), flat),
            pl.BlockSpec((TOP_K, tm), flat),
            pl.BlockSpec((n_exp, LANES), const2),
        ),
        out_shape=(
            jax.ShapeDtypeStruct((b, s, d), F32),
            jax.ShapeDtypeStruct((b, s, d // 2), jnp.uint32),
            jax.ShapeDtypeStruct((TOP_K, t_all), I32),
            jax.ShapeDtypeStruct((SUBLANES, t_all), F32),
            jax.ShapeDtypeStruct((TOP_K, t_all), I32),
            jax.ShapeDtypeStruct((n_exp, LANES), I32),
        ),
        scratch_shapes=[pltpu.VMEM((n_exp, LANES), F32)],
        compiler_params=_params("arbitrary", "arbitrary"),
        name="output_projection_router",
    )(y_pool, y_nsa, w_out, x, gate1, norm_g, scale, shift, w_router_t, b_router, tri)


def _dest_kernel(ps_ref, idx_ref, rank_ref, dest_ref):
    idx = idx_ref[...]
    dest = rank_ref[...]
    for e in range(ps_ref.shape[0]):
        dest = dest + jnp.where(idx == e, ps_ref[e], 0)
    dest_ref[...] = dest


def _destinations(seg_starts, idx_t, rank_t):
    return pl.pallas_call(
        _dest_kernel,
        in_specs=[
            pl.BlockSpec(memory_space=pltpu.SMEM),
            pl.BlockSpec(memory_space=pltpu.VMEM),
            pl.BlockSpec(memory_space=pltpu.VMEM),
        ],
        out_specs=pl.BlockSpec(memory_space=pltpu.VMEM),
        out_shape=jax.ShapeDtypeStruct(idx_t.shape, I32),
        name="moe_destinations",
    )(seg_starts, idx_t, rank_t)


def _row_copy(src_ref, src_row, dst_ref, dst_row, sem):
    return pltpu.make_async_copy(src_ref.at[pl.ds(src_row, 1)], dst_ref.at[pl.ds(dst_row, 1)], sem)


def _dispatch_kernel(dest_ref, starts_ref, ends_ref, h_ref, xs_ref, zero_ref, sem, zsem):
    tm = h_ref.shape[0]
    n_exp = starts_ref.shape[0]
    zrows = zero_ref.shape[0]

    @pl.when(pl.program_id(0) == 0)
    def _():
        zero_ref[...] = jnp.zeros(zero_ref.shape, zero_ref.dtype)

        def clear_block(end_row, start):
            for r in range(EXPERT_TM // zrows):
                row0 = pl.multiple_of(end_row - EXPERT_TM + r * zrows, zrows)
                cp = pltpu.make_async_copy(zero_ref, xs_ref.at[pl.ds(row0, zrows)], zsem)
                cp.start() if start else cp.wait()

        def clear(e, start):
            @pl.when(ends_ref[e] > starts_ref[e])
            def _():
                clear_block(ends_ref[e], start)

        first_unused = ends_ref[n_exp - 1] // EXPERT_TM
        n_blk = xs_ref.shape[0] // EXPERT_TM
        for start in (True, False):
            lax.fori_loop(0, n_exp, lambda e, c: (clear(e, start), c)[1], 0)
            lax.fori_loop(first_unused, n_blk, lambda i, c: (clear_block((i + 1) * EXPERT_TM, start), c)[1], 0)

    def issue(t, carry):
        for k in range(TOP_K):
            _row_copy(h_ref, t, xs_ref, dest_ref[k, t], sem).start()
        return carry

    lax.fori_loop(0, tm, issue, 0)

    def drain(t, carry):
        for k in range(TOP_K):
            _row_copy(h_ref, t, xs_ref, dest_ref[k, t], sem).wait()
        return carry

    lax.fori_loop(0, tm, drain, 0)


def _dispatch(dest_t, seg_starts, seg_ends, h_rows, n_buf):
    t_all, d = h_rows.shape
    tm = DISPATCH_TM
    return pl.pallas_call(
        _dispatch_kernel,
        grid=(t_all // tm,),
        in_specs=[
            pl.BlockSpec((TOP_K, tm), lambda i: (0, i), memory_space=pltpu.SMEM),
            pl.BlockSpec(memory_space=pltpu.SMEM),
            pl.BlockSpec(memory_space=pltpu.SMEM),
            pl.BlockSpec((tm, d), lambda i: (i, 0)),
        ],
        out_specs=pl.BlockSpec(memory_space=pl.ANY),
        out_shape=jax.ShapeDtypeStruct((n_buf, d), h_rows.dtype),
        scratch_shapes=[pltpu.VMEM((DISPATCH_ZERO_ROWS, d), h_rows.dtype),
                        pltpu.SemaphoreType.DMA(()), pltpu.SemaphoreType.DMA(())],
        compiler_params=_params("arbitrary"),
        name="moe_dispatch",
    )(dest_t, seg_starts, seg_ends, h_rows)


def _ffn_up_kernel(be_ref, nu_ref, xs_ref, wg_ref, bg_ref, wu_ref, bu_ref, h_ref):
    del be_ref
    used = pl.program_id(1) < nu_ref[0]

    @pl.when(used)
    def _():
        first, second = _unpack_bf16_pair(xs_ref[...])
        xb = jnp.concatenate([first.astype(BF16), second.astype(BF16)], axis=1)
        gl = jnp.minimum(_dot(xb, wg_ref[0, 0].astype(BF16)) + bg_ref[0, 0], SWIGLU_LIMIT)
        lin = jnp.clip(_dot(xb, wu_ref[0, 0].astype(BF16)) + bu_ref[0, 0], -SWIGLU_LIMIT, SWIGLU_LIMIT)
        h_ref[...] = (gl * jax.nn.sigmoid(SWIGLU_ALPHA * gl) * (lin + 1.0)).astype(BF16)

    @pl.when(jnp.logical_not(used))
    def _():
        h_ref[...] = jnp.zeros(h_ref.shape, BF16)


def _ffn_down_kernel(be_ref, nu_ref, h_ref, wd_ref, bd_ref, y_ref):
    del be_ref
    used = pl.program_id(1) < nu_ref[0]

    @pl.when(used)
    def _():
        y = _dot(h_ref[...], wd_ref[0, 0].astype(BF16)) + bd_ref[0, 0]
        half = y.shape[1] // 2
        y_ref[...] = _pack_bf16_pair(y[:, 0:half], y[:, half:2 * half])

    @pl.when(jnp.logical_not(used))
    def _():
        y_ref[...] = jnp.zeros(y_ref.shape, jnp.uint32)


def _expert_ffn(layer, block_expert, n_used, xs, w_gate, b_gate, w_up, b_up, w_down, b_down):
    n_buf = xs.shape[0]
    depth, n_exp, d, f = w_gate.shape
    tm, tf, td = EXPERT_TM, EXPERT_TF, EXPERT_TD
    n_blk = n_buf // tm

    def rows(j, i, be, nu):
        return (jnp.minimum(i, nu[0] - 1), 0)

    def cols(j, i, be, nu):
        return (layer, be[i], 0, j)

    def out_block(j, i, be, nu):
        return (i, j)

    hidden = pl.pallas_call(
        _ffn_up_kernel,
        grid_spec=pltpu.PrefetchScalarGridSpec(
            num_scalar_prefetch=2,
            grid=(f // tf, n_blk),
            in_specs=[
                pl.BlockSpec((tm, d // 2), rows),
                pl.BlockSpec((1, 1, d, tf), cols),
                pl.BlockSpec((1, 1, 1, tf), cols),
                pl.BlockSpec((1, 1, d, tf), cols),
                pl.BlockSpec((1, 1, 1, tf), cols),
            ],
            out_specs=pl.BlockSpec((tm, tf), out_block),
        ),
        out_shape=jax.ShapeDtypeStruct((n_buf, f), BF16),
        compiler_params=_params("arbitrary", "arbitrary"),
        name="moe_ffn_up",
    )(block_expert, n_used, xs, w_gate, b_gate.reshape(depth, n_exp, 1, f), w_up, b_up.reshape(depth, n_exp, 1, f))
    return pl.pallas_call(
        _ffn_down_kernel,
        grid_spec=pltpu.PrefetchScalarGridSpec(
            num_scalar_prefetch=2,
            grid=(d // td, n_blk),
            in_specs=[
                pl.BlockSpec((tm, f), rows),
                pl.BlockSpec((1, 1, f, td), cols),
                pl.BlockSpec((1, 1, 1, td), cols),
            ],
            out_specs=pl.BlockSpec((tm, td // 2), out_block),
        ),
        out_shape=jax.ShapeDtypeStruct((n_buf, d // 2), jnp.uint32),
        compiler_params=_params("arbitrary", "arbitrary"),
        name="moe_ffn_down",
    )(block_expert, n_used, hidden, w_down, b_down.reshape(depth, n_exp, 1, d))


def _combine_kernel(dest_ref, wt_ref, x1_ref, g2_ref, fg_ref, y_ref, o_ref, rows_ref, sem, *, final_norm):
    tm = x1_ref.shape[1]

    def issue(t, carry):
        for k in range(TOP_K):
            _row_copy(y_ref, dest_ref[k, t], rows_ref.at[k], t, sem).start()
        return carry

    lax.fori_loop(0, tm, issue, 0)

    wt = jnp.concatenate([wt_ref[...], jnp.zeros((tm - SUBLANES, tm), F32)], axis=0).T

    def drain(t, carry):
        for k in range(TOP_K):
            _row_copy(y_ref, dest_ref[k, t], rows_ref.at[k], t, sem).wait()
        return carry

    lax.fori_loop(0, tm, drain, 0)

    acc_first, acc_second = None, None
    for k in range(TOP_K):
        first, second = _unpack_bf16_pair(rows_ref[k])
        w = wt[:, k:k + 1]
        acc_first = w * first if k == 0 else acc_first + w * first
        acc_second = w * second if k == 0 else acc_second + w * second
    half = EXPERT_TD // 2
    pieces = []
    for j in range(acc_first.shape[1] // half):
        pieces += [acc_first[:, j * half:(j + 1) * half], acc_second[:, j * half:(j + 1) * half]]
    out = x1_ref[0] + g2_ref[0] * jnp.concatenate(pieces, axis=1)
    if final_norm:
        out = _rmsnorm(out, fg_ref[...])
    o_ref[0] = out


def _combine(dest_t, wt_t, x1, gate2, final_g, y_rows, final_norm):
    b, s, d = x1.shape
    tm = COMBINE_TM
    flat = lambda bi, m: (0, bi * (s // tm) + m)
    return pl.pallas_call(
        functools.partial(_combine_kernel, final_norm=final_norm),
        grid=(b, s // tm),
        in_specs=[
            pl.BlockSpec((TOP_K, tm), flat, memory_space=pltpu.SMEM),
            pl.BlockSpec((SUBLANES, tm), flat),
            pl.BlockSpec((1, tm, d), lambda bi, m: (bi, m, 0)),
            pl.BlockSpec((1, 1, d), lambda bi, m: (bi, 0, 0)),
            pl.BlockSpec((1, d), lambda bi, m: (0, 0)),
            pl.BlockSpec(memory_space=pl.ANY),
        ],
        out_specs=pl.BlockSpec((1, tm, d), lambda bi, m: (bi, m, 0)),
        out_shape=jax.ShapeDtypeStruct((b, s, d), F32),
        scratch_shapes=[pltpu.VMEM((TOP_K, tm, d // 2), jnp.uint32), pltpu.SemaphoreType.DMA(())],
        compiler_params=_params("arbitrary", "arbitrary"),
        name="moe_combine",
    )(dest_t, wt_t, x1, gate2, final_g, y_rows)


def _segment_layout(counts, n_blk):
    padded = (counts + EXPERT_TM - 1) // EXPERT_TM * EXPERT_TM
    ends = jnp.cumsum(padded)
    starts = ends - padded
    n_used = ends[-1] // EXPERT_TM
    blk = jnp.minimum(jnp.arange(n_blk, dtype=I32), n_used - 1) * EXPERT_TM
    block_expert = jnp.minimum(jnp.searchsorted(ends, blk, side="right"), counts.shape[0] - 1)
    return starts.astype(I32), ends.astype(I32), block_expert.astype(I32), n_used.reshape(1).astype(I32)


def kernel(x, c, norm1_g, norm2_g, w_ada, b_ada, w_in, pool_w, pool_scale, cmp_pos_k, cmp_w1_k, cmp_b1_k, cmp_w2_k, cmp_b2_k, cmp_pos_v, cmp_w1_v, cmp_b1_v, cmp_w2_v, cmp_b2_v, w_out, w_router, b_router, w_gate, b_gate, w_up, b_up, w_down, b_down, final_g):
    b, s, d = x.shape
    depth = w_ada.shape[0]
    n_exp = w_router.shape[-1]
    t_all = b * s
    assert s % PROJ_TM == 0 and s % SLC_CHUNK == 0 and t_all % DISPATCH_TM == 0
    n_main = POOL_WIDTH + NSA_WIDTH + 6 * KV_WIDTH
    assert w_in.shape[-1] == n_main + N_GATE_COLS

    c_pad = jnp.pad(c, ((0, SUBLANES - b), (0, 0)))
    mod = _modulation(c_pad, w_ada, b_ada)[:, :b]
    tri = (jnp.arange(PROJ_TM)[:, None] < jnp.arange(PROJ_TM)[None, :]).astype(BF16)
    n_buf = (t_all * TOP_K + n_exp * (EXPERT_TM - 1)) // EXPERT_TM * EXPERT_TM
    n_blk = n_buf // EXPERT_TM

    for l in range(depth):
        sh1, sc1, g1, sh2, sc2, g2 = [m.reshape(b, 1, d) for m in jnp.split(mod[l], 6, axis=-1)]
        w_main = w_in[l, :, :n_main].astype(BF16)
        w_gates = jnp.pad(w_in[l, :, n_main:], ((0, 0), (0, LANES - N_GATE_COLS))).astype(BF16)
        u_pool, qT, kvc, ksl, vslT, kwin, vwinT, gT = _input_projection(
            x, norm1_g[l].reshape(1, d), sc1, sh1, w_main, w_gates)
        y_pool = _pool_mixer(u_pool, pool_w[l].astype(BF16), pool_scale[l].reshape(1, POOL_WIDTH))
        kc, kcT = _compress(
            kvc,
            jnp.stack([cmp_pos_k[l], cmp_pos_v[l]]).reshape(2, 1, CMP_BLOCK * HEAD_DIM),
            jnp.stack([cmp_w1_k[l], cmp_w1_v[l]]).astype(BF16),
            jnp.stack([cmp_b1_k[l], cmp_b1_v[l]])[:, None, :],
            jnp.stack([cmp_w2_k[l], cmp_w2_v[l]]).astype(BF16),
            jnp.stack([cmp_b2_k[l], cmp_b2_v[l]])[:, None, :])
        y_nsa = _nsa(qT, kc, kcT, ksl, vslT, kwin, vwinT, gT)
        x1, h2, idx_t, wt_t, rank_t, counts = _output_projection(
            y_pool, y_nsa, w_out[l].astype(BF16), x, g1, norm2_g[l].reshape(1, d), sc2, sh2,
            w_router[l].T.astype(BF16), b_router[l].reshape(n_exp, 1), tri)
        seg_starts, seg_ends, block_expert, n_used = _segment_layout(counts[:, 0], n_blk)
        dest_t = _destinations(seg_starts, idx_t, rank_t)
        xs = _dispatch(dest_t, seg_starts, seg_ends, h2.reshape(t_all, d // 2), n_buf)
        y_rows = _expert_ffn(l, block_expert, n_used, xs, w_gate, b_gate, w_up, b_up, w_down, b_down)
        x = _combine(dest_t, wt_t, x1, g2, final_g.reshape(1, d), y_rows, final_norm=(l == depth - 1))
    return x
```

```python
import functools

import jax
import jax.numpy as jnp
from jax import lax
from jax.experimental import pallas as pl
from jax.experimental.pallas import tpu as pltpu

F32 = jnp.float32
BF16 = jnp.bfloat16
I32 = jnp.int32

HEAD_DIM = 128
N_HEADS = 8
N_KV_GROUPS = 2
HEADS_PER_GROUP = N_HEADS // N_KV_GROUPS
POOL_WINDOWS = (2, 4, 8, 16)
POOL_GROUP = 256
POOL_WIDTH = POOL_GROUP * len(POOL_WINDOWS)
NSA_WIDTH = N_HEADS * HEAD_DIM
KV_WIDTH = N_KV_GROUPS * HEAD_DIM
N_GATE_COLS = 3 * N_HEADS
CMP_BLOCK = 32
CMP_STRIDE = 16
SEL_BLOCK = 64
SEL_RATIO = SEL_BLOCK // CMP_STRIDE
N_SELECT = 16
WINDOW = 512
Q_BLOCK = 128
TOP_K = 4
SWIGLU_ALPHA = 1.702
SWIGLU_LIMIT = 7.0
NORM_EPS = 1e-5
NEG = -1e30
BIG = 1e30
BELOW_NEG = -3e38
V_ROWS = HEAD_DIM + 16
LOG2_E = 1.4426950408889634

LANES = 128
SUBLANES = 8
VMEM_LIMIT_BYTES = 56 * 1024 * 1024

MOD_TN = 1024
PROJ_TM = 512
SLC_CHUNK = 256
SLC_UNROLL = 6
WIN_CHUNK = 128
DISPATCH_TM = 256
DISPATCH_ZERO_ROWS = 256
COMBINE_TM = 128
EXPERT_TM = 1024
EXPERT_TF = 512
EXPERT_TD = 1024


def _params(*sem):
    return pltpu.CompilerParams(dimension_semantics=sem, vmem_limit_bytes=VMEM_LIMIT_BYTES)


def _dot(a, b):
    return jnp.dot(a, b, preferred_element_type=F32)


def _rmsnorm(x, g):
    return x * lax.rsqrt(jnp.mean(x * x, axis=-1, keepdims=True) + NORM_EPS) * g


_HIGH_HALF = 0xFFFF0000


def _pack_bf16_pair(a, b):
    hi = pltpu.bitcast(a.astype(BF16).astype(F32), jnp.uint32) & jnp.uint32(_HIGH_HALF)
    lo = pltpu.bitcast(b.astype(BF16).astype(F32), jnp.uint32) >> 16
    return hi | lo


def _unpack_bf16_pair(w):
    return pltpu.bitcast(w & jnp.uint32(_HIGH_HALF), F32), pltpu.bitcast(w << 16, F32)


def _mod_kernel(c_ref, w_ref, b_ref, o_ref):
    c = c_ref[...]
    ca = (c * jax.nn.sigmoid(c)).astype(BF16)
    o_ref[0] = _dot(ca, w_ref[0].astype(BF16)) + b_ref[0]


def _modulation(c_pad, w_ada, b_ada):
    depth, d, n = w_ada.shape
    rows = c_pad.shape[0]
    return pl.pallas_call(
        _mod_kernel,
        grid=(depth, n // MOD_TN),
        in_specs=[
            pl.BlockSpec((rows, d), lambda l, j: (0, 0)),
            pl.BlockSpec((1, d, MOD_TN), lambda l, j: (l, 0, j)),
            pl.BlockSpec((1, 1, MOD_TN), lambda l, j: (l, 0, j)),
        ],
        out_specs=pl.BlockSpec((1, rows, MOD_TN), lambda l, j: (l, 0, j)),
        out_shape=jax.ShapeDtypeStruct((depth, rows, n), F32),
        compiler_params=_params("arbitrary", "arbitrary"),
        name="adaln_modulation",
    )(c_pad, w_ada, b_ada.reshape(depth, 1, n))


def _inproj_kernel(x_ref, g_ref, sc_ref, sh_ref, w_ref, wg_ref,
                   upool_ref, qT_ref, kvc_ref, ksl_ref, vslT_ref, kwin_ref, vwinT_ref, gT_ref):
    tm = x_ref.shape[1]
    h = _rmsnorm(x_ref[0], g_ref[...]) * (1.0 + sc_ref[0]) + sh_ref[0]
    hb = h.astype(BF16)
    upool_ref[0] = _dot(hb, w_ref[:, 0:POOL_WIDTH])
    q0 = POOL_WIDTH
    scale = HEAD_DIM ** -0.5 * LOG2_E
    def four_heads(col0):
        wide = _dot(hb, w_ref[:, col0:col0 + 4 * HEAD_DIM])
        return [wide[:, i * HEAD_DIM:(i + 1) * HEAD_DIM] for i in range(4)]

    for j in range(N_HEADS // 4):
        for i, q in enumerate(four_heads(q0 + j * 4 * HEAD_DIM)):
            hd = 4 * j + i
            qT_ref[0, hd * HEAD_DIM:(hd + 1) * HEAD_DIM, :] = (q * scale).T.astype(BF16)
    c0 = q0 + NSA_WIDTH
    for i, kv in enumerate(four_heads(c0)):
        kvc_ref[0, i] = kv
    s0 = c0 + 2 * KV_WIDTH
    w0 = s0 + 2 * KV_WIDTH
    slc = four_heads(s0)
    win = four_heads(w0)
    tpos = pl.program_id(1) * tm + lax.broadcasted_iota(I32, (tm, HEAD_DIM), 0)
    lane = lax.broadcasted_iota(I32, (tm, HEAD_DIM), 1)
    block_onehot = jnp.where(lane == ((tpos // SEL_BLOCK) & (LANES - 1)), 1.0, 0.0).astype(BF16)
    for g in range(N_KV_GROUPS):
        ksl_ref[0, g, :, 0:HEAD_DIM] = slc[g].astype(BF16)
        ksl_ref[0, g, :, HEAD_DIM:2 * HEAD_DIM] = block_onehot
        vT = slc[N_KV_GROUPS + g].T.astype(BF16)
        for c in range(tm // SLC_CHUNK):
            vslT_ref[0, g, c, 0:HEAD_DIM, :] = vT[:, c * SLC_CHUNK:(c + 1) * SLC_CHUNK]
            vslT_ref[0, g, c, HEAD_DIM:V_ROWS, :] = jnp.ones((V_ROWS - HEAD_DIM, SLC_CHUNK), BF16)
        kwin_ref[0, g] = win[g].astype(BF16)
        vT = win[N_KV_GROUPS + g].T.astype(BF16)
        for c in range(tm // WIN_CHUNK):
            vwinT_ref[0, g, c, 0:HEAD_DIM, :] = vT[:, c * WIN_CHUNK:(c + 1) * WIN_CHUNK]
            vwinT_ref[0, g, c, HEAD_DIM:V_ROWS, :] = jnp.ones((V_ROWS - HEAD_DIM, WIN_CHUNK), BF16)
    gl = jax.nn.sigmoid(_dot(hb, wg_ref[...]))
    gT_ref[0] = gl.T[0:N_GATE_COLS, :]


def _input_projection(x, norm_g, scale, shift, w_main, w_gates):
    b, s, d = x.shape
    tm = PROJ_TM
    n_main = w_main.shape[1]
    tok = lambda bi, m: (bi, m, 0)
    per_b = lambda bi, m: (bi, 0, 0)
    const2 = lambda bi, m: (0, 0)
    out_shape = (
        jax.ShapeDtypeStruct((b, s, POOL_WIDTH), F32),
        jax.ShapeDtypeStruct((b, NSA_WIDTH, s), BF16),
        jax.ShapeDtypeStruct((b, 2 * N_KV_GROUPS, s, HEAD_DIM), F32),
        jax.ShapeDtypeStruct((b, N_KV_GROUPS, s, 2 * HEAD_DIM), BF16),
        jax.ShapeDtypeStruct((b, N_KV_GROUPS, s // SLC_CHUNK, V_ROWS, SLC_CHUNK), BF16),
        jax.ShapeDtypeStruct((b, N_KV_GROUPS, s, HEAD_DIM), BF16),
        jax.ShapeDtypeStruct((b, N_KV_GROUPS, s // WIN_CHUNK, V_ROWS, WIN_CHUNK), BF16),
        jax.ShapeDtypeStruct((b, N_GATE_COLS, s), F32),
    )
    out_specs = (
        pl.BlockSpec((1, tm, POOL_WIDTH), tok),
        pl.BlockSpec((1, NSA_WIDTH, tm), lambda bi, m: (bi, 0, m)),
        pl.BlockSpec((1, 2 * N_KV_GROUPS, tm, HEAD_DIM), lambda bi, m: (bi, 0, m, 0)),
        pl.BlockSpec((1, N_KV_GROUPS, tm, 2 * HEAD_DIM), lambda bi, m: (bi, 0, m, 0)),
        pl.BlockSpec((1, N_KV_GROUPS, tm // SLC_CHUNK, V_ROWS, SLC_CHUNK), lambda bi, m: (bi, 0, m, 0, 0)),
        pl.BlockSpec((1, N_KV_GROUPS, tm, HEAD_DIM), lambda bi, m: (bi, 0, m, 0)),
        pl.BlockSpec((1, N_KV_GROUPS, tm // WIN_CHUNK, V_ROWS, WIN_CHUNK), lambda bi, m: (bi, 0, m, 0, 0)),
        pl.BlockSpec((1, N_GATE_COLS, tm), lambda bi, m: (bi, 0, m)),
    )
    return pl.pallas_call(
        _inproj_kernel,
        grid=(b, s // tm),
        in_specs=[
            pl.BlockSpec((1, tm, d), tok),
            pl.BlockSpec((1, d), const2),
            pl.BlockSpec((1, 1, d), per_b),
            pl.BlockSpec((1, 1, d), per_b),
            pl.BlockSpec((d, n_main), const2, pipeline_mode=pl.Buffered(1)),
            pl.BlockSpec((d, LANES), const2, pipeline_mode=pl.Buffered(1)),
        ],
        out_specs=out_specs,
        out_shape=out_shape,
        compiler_params=_params("arbitrary", "arbitrary"),
        name="input_projection",
    )(x, norm_g, scale, shift, w_main, w_gates)


def _pool_kernel(u_ref, pw_ref, ps_ref, o_ref, ext_ref):
    tm = u_ref.shape[1]
    halo = max(POOL_WINDOWS)
    m = pl.program_id(1)

    @pl.when(m == 0)
    def _():
        ext_ref[0:halo, :] = jnp.zeros((halo, POOL_WIDTH), F32)

    ext_ref[halo:halo + tm, :] = u_ref[0]
    t = m * tm + lax.broadcasted_iota(I32, (tm, 1), 0)
    for gi, w in enumerate(POOL_WINDOWS):
        cols = slice(gi * POOL_GROUP, (gi + 1) * POOL_GROUP)
        ug = ext_ref[halo:halo + tm, cols]
        acc = ug
        for i in range(1, w):
            acc = acc + ext_ref[halo - i:halo - i + tm, cols]
        cnt = jnp.minimum(t + 1, w).astype(F32)
        diff = (acc / cnt - ug).astype(BF16)
        o_ref[0, :, cols] = (_dot(diff, pw_ref[gi]) * ps_ref[:, cols]).astype(BF16)
    ext_ref[0:halo, :] = ext_ref[tm:tm + halo, :]


def _pool_mixer(u_pool, pool_w, pool_scale):
    b, s, _ = u_pool.shape
    tm = PROJ_TM
    return pl.pallas_call(
        _pool_kernel,
        grid=(b, s // tm),
        in_specs=[
            pl.BlockSpec((1, tm, POOL_WIDTH), lambda bi, m: (bi, m, 0)),
            pl.BlockSpec((len(POOL_WINDOWS), POOL_GROUP, POOL_GROUP), lambda bi, m: (0, 0, 0)),
            pl.BlockSpec((1, POOL_WIDTH), lambda bi, m: (0, 0)),
        ],
        out_specs=pl.BlockSpec((1, tm, POOL_WIDTH), lambda bi, m: (bi, m, 0)),
        out_shape=jax.ShapeDtypeStruct((b, s, POOL_WIDTH), BF16),
        scratch_shapes=[pltpu.VMEM((tm + max(POOL_WINDOWS), POOL_WIDTH), F32)],
        compiler_params=_params("arbitrary", "arbitrary"),
        name="pool_mixer",
    )(u_pool, pool_w, pool_scale)


def _compress_kernel(a_ref, pos_ref, w1_ref, b1_ref, w2_ref, b2_ref, o_ref, oT_ref):
    half = CMP_STRIDE * HEAD_DIM
    a = a_ref[0, 0]
    nb = a.shape[0]
    pos = pos_ref[0]
    a_lo = (a + pos[:, 0:half]).astype(BF16)
    a_hi = (a + pos[:, half:2 * half]).astype(BF16)
    p_lo = _dot(a_lo, w1_ref[0, 0:half, :])
    p_hi = _dot(a_hi, w1_ref[0, half:2 * half, :])
    hid = jax.nn.gelu(p_lo + pltpu.roll(p_hi, nb - 1, 0) + b1_ref[0])
    out = _dot(hid.astype(BF16), w2_ref[0]) + b2_ref[0]
    o_ref[0, 0] = out.astype(BF16)
    oT_ref[0, 0] = out.T.astype(BF16)


def _compress(kvc, pos, w1, b1, w2, b2):
    b, n4, s, _ = kvc.shape
    nb = s // CMP_STRIDE
    a = kvc.reshape(b, n4, nb, CMP_STRIDE * HEAD_DIM)
    hidden = w1.shape[-1]
    kv = lambda bi, i: (i // N_KV_GROUPS, 0, 0)
    return pl.pallas_call(
        _compress_kernel,
        grid=(b, n4),
        in_specs=[
            pl.BlockSpec((1, 1, nb, CMP_STRIDE * HEAD_DIM), lambda bi, i: (bi, i, 0, 0)),
            pl.BlockSpec((1, 1, CMP_BLOCK * HEAD_DIM), kv),
            pl.BlockSpec((1, CMP_BLOCK * HEAD_DIM, hidden), kv),
            pl.BlockSpec((1, 1, hidden), kv),
            pl.BlockSpec((1, hidden, HEAD_DIM), kv),
            pl.BlockSpec((1, 1, HEAD_DIM), kv),
        ],
        out_specs=(
            pl.BlockSpec((1, 1, nb, HEAD_DIM), lambda bi, i: (bi, i, 0, 0)),
            pl.BlockSpec((1, 1, HEAD_DIM, nb), lambda bi, i: (bi, i, 0, 0)),
        ),
        out_shape=(
            jax.ShapeDtypeStruct((b, n4, nb, HEAD_DIM), BF16),
            jax.ShapeDtypeStruct((b, n4, HEAD_DIM, nb), BF16),
        ),
        compiler_params=_params("arbitrary", "arbitrary"),
        name="kv_compress",
    )(a, pos, w1, b1, w2, b2)


def _flash_step(s, v_t, m_old, acc_ref):
    m_new = jnp.maximum(m_old, jnp.max(s, axis=0, keepdims=True))
    alpha = jnp.exp2(m_old - m_new)
    p = jnp.exp2(s - m_new)
    acc_ref[...] = acc_ref[...] * alpha + _dot(v_t, p.astype(BF16))
    return m_new


def _nsa_kernel(qT_ref, kc_ref, vcT_ref, ksl_ref, vslT_ref, kwin_ref, vwinT_ref, gT_ref, y_ref,
                qcat_ref, pbuf_ref, score_ref, sel_ref, wq_ref, sa_ref, sb_ref, sc_ref, ml_ref, wbuf_ref,
                ocmp_ref, aslc_ref, awin_ref):
    n = pl.program_id(2)
    q_start = n * Q_BLOCK
    nq = HEADS_PER_GROUP * Q_BLOCK
    nb = kc_ref.shape[2]
    ns = sel_ref.shape[0]

    for r in range(HEADS_PER_GROUP):
        qcat_ref[:, r * Q_BLOCK:(r + 1) * Q_BLOCK] = qT_ref[0, r * HEAD_DIM:(r + 1) * HEAD_DIM, :]
    qc = qcat_ref[...]
    tin = lax.broadcasted_iota(I32, (1, nq), 1) & (Q_BLOCK - 1)
    tok = q_start + tin

    sel_ref[...] = jnp.full((ns, Q_BLOCK), NEG, F32)
    t1 = q_start + lax.broadcasted_iota(I32, (1, Q_BLOCK), 1)
    tj = t1 // SEL_BLOCK

    def compress_and_select(rows):
        srows = rows // SEL_RATIO
        s = _dot(kc_ref[0, 0, 0:rows, :], qc)
        ci = lax.broadcasted_iota(I32, (rows, nq), 0)
        s = jnp.where(ci * CMP_STRIDE + (CMP_BLOCK - 1) <= tok, s, NEG)
        m = jnp.max(s, axis=0, keepdims=True)
        e = jnp.exp2(s - m)
        l = jnp.sum(e, axis=0, keepdims=True)
        inv = jnp.where(tok >= CMP_BLOCK - 1, 1.0 / l, 0.0)
        p = e * inv
        ocmp_ref[...] = _dot(vcT_ref[0, 0, :, 0:rows], p.astype(BF16))

        psum = p[:, 0:Q_BLOCK]
        for r in range(1, HEADS_PER_GROUP):
            psum = psum + p[:, r * Q_BLOCK:(r + 1) * Q_BLOCK]
        pbuf_ref[0:SUBLANES, :] = jnp.zeros((SUBLANES, Q_BLOCK), F32)
        pbuf_ref[SUBLANES:SUBLANES + rows, :] = psum
        imp = pbuf_ref[pl.ds(SUBLANES - 1, srows, stride=SEL_RATIO), :]
        for d in range(SEL_RATIO):
            imp = imp + pbuf_ref[pl.ds(SUBLANES + d, srows, stride=SEL_RATIO), :]

        ji = lax.broadcasted_iota(I32, (srows, Q_BLOCK), 0)
        forced = (ji == 0) | (ji == tj) | (ji == tj - 1)
        sel_ref[0:srows, :] = jnp.where(forced, 0.0, NEG)
        score_ref[0:srows, :] = jnp.where(forced, BELOW_NEG, jnp.where(ji * SEL_BLOCK <= t1, imp, NEG))
        n_forced = 1 + (tj > 0).astype(I32) + (tj > 1).astype(I32)
        quota = min(N_SELECT, srows) - n_forced

        def pick(it, carry):
            sc = score_ref[0:srows, :]
            mx = jnp.max(sc, axis=0, keepdims=True)
            first = jnp.min(jnp.where(sc == mx, ji, srows), axis=0, keepdims=True)
            hit = ji == first
            sel_ref[0:srows, :] = jnp.where(hit & (mx >= 0.0) & (it < quota), 0.0, sel_ref[0:srows, :])
            score_ref[0:srows, :] = jnp.where(hit, BELOW_NEG, sc)
            return carry

        n_picks = min(N_SELECT, srows) - jnp.where(n == 0, 1, 3)
        lax.fori_loop(0, n_picks, pick, 0)

    per_q = Q_BLOCK // CMP_STRIDE
    classes = sorted({r for r in (nb // 4, nb // 2) if r % (SEL_RATIO * SUBLANES * 2) == 0} | {nb})
    for k, rows in enumerate(classes):
        lo = 0 if k == 0 else classes[k - 1] // per_q
        in_class = (n >= lo) if k == len(classes) - 1 else ((n >= lo) & (n < rows // per_q))
        pl.when(in_class)(functools.partial(compress_and_select, rows))

    sel = sel_ref[...]
    wq_ref[wq_ref.shape[0] - 1, 0:HEAD_DIM, :] = jnp.zeros((HEAD_DIM, nq), BF16)
    wq_ref[wq_ref.shape[0] - 1, HEAD_DIM:HEAD_DIM + LANES, :] = jnp.full((LANES, nq), NEG, BF16)
    for h in range(wq_ref.shape[0] - 1):
        rows_h = min(LANES, ns - LANES * h)
        wq_ref[h, 0:HEAD_DIM, :] = qc
        blk = sel[LANES * h:LANES * h + rows_h].astype(BF16)
        for r in range(HEADS_PER_GROUP):
            wq_ref[h, HEAD_DIM:HEAD_DIM + rows_h, r * Q_BLOCK:(r + 1) * Q_BLOCK] = blk
        if rows_h < LANES:
            wq_ref[h, HEAD_DIM + rows_h:HEAD_DIM + LANES, :] = jnp.zeros((LANES - rows_h, nq), BF16)

    m0 = jnp.full((1, nq), NEG, F32)

    aslc_ref[...] = jnp.zeros(aslc_ref.shape, F32)
    blocks_per_chunk = SLC_CHUNK // SEL_BLOCK

    ml_ref[0:1, :] = m0

    n_past = q_start // SLC_CHUNK
    last_chunk = ksl_ref.shape[2] // SLC_CHUNK - 1
    n_half = wq_ref.shape[0] - 1

    def slc_scores(c, buf_ref):
        slot = jnp.where(c < n_past, (c * blocks_per_chunk) // LANES, n_half)
        k0 = pl.multiple_of(jnp.minimum(c, last_chunk) * SLC_CHUNK, SLC_CHUNK)
        buf_ref[...] = _dot(ksl_ref[0, 0, pl.ds(k0, SLC_CHUNK), :], wq_ref[slot])

    def slc_flash(c, buf_ref):
        v_t = vslT_ref[0, 0, jnp.minimum(c, last_chunk)]
        ml_ref[0:1, :] = _flash_step(buf_ref[...], v_t, ml_ref[0:1, :], aslc_ref)

    def slc_round(i, carry):
        bufs = (sa_ref, sb_ref, sc_ref)
        for k in range(SLC_UNROLL):
            slc_scores(SLC_UNROLL * i + k + 2, bufs[(k + 2) % 3])
            slc_flash(SLC_UNROLL * i + k, bufs[k % 3])
        return carry

    slc_scores(0, sa_ref)
    slc_scores(1, sb_ref)
    lax.fori_loop(0, (n_past + SLC_UNROLL - 1) // SLC_UNROLL, slc_round, 0)

    k0 = pl.multiple_of(n_past * SLC_CHUNK, SLC_CHUNK)
    sd = _dot(ksl_ref[0, 0, pl.ds(k0, SLC_CHUNK), :], wq_ref[(n_past * blocks_per_chunk) // LANES])
    kpos = k0 + lax.broadcasted_iota(I32, (SLC_CHUNK, nq), 0)
    sd = jnp.where(kpos <= tok, sd, NEG)
    _flash_step(sd, vslT_ref[0, 0, n_past], ml_ref[0:1, :], aslc_ref)

    n_wc = WINDOW // WIN_CHUNK + 1
    kk = lax.broadcasted_iota(I32, (WIN_CHUNK, nq), 0)
    chunk_ids = []
    m_win = m0
    for i in range(n_wc):
        wb = n - (n_wc - 1) + i
        wbc = jnp.maximum(wb, 0)
        chunk_ids.append(wbc)
        k0 = pl.multiple_of(wbc * WIN_CHUNK, WIN_CHUNK)
        sc = _dot(kwin_ref[0, 0, pl.ds(k0, WIN_CHUNK), :], qc)
        if i == 0:
            sc = jnp.where(kk > tin, sc, NEG)
        if i == n_wc - 1:
            sc = jnp.where(kk <= tin, sc, NEG)
        else:
            sc = sc + jnp.where(wb >= 0, 0.0, NEG)
        wbuf_ref[i] = sc
        m_win = jnp.maximum(m_win, jnp.max(sc, axis=0, keepdims=True))
    for i in range(n_wc):
        p = jnp.exp2(wbuf_ref[i] - m_win)
        pv = _dot(vwinT_ref[0, 0, chunk_ids[i]], p.astype(BF16))
        if i == 0:
            awin_ref[...] = pv
        else:
            awin_ref[...] = awin_ref[...] + pv

    gates = gT_ref[0, 0]
    inv_slc = 1.0 / aslc_ref[HEAD_DIM:HEAD_DIM + 1, :]
    inv_win = 1.0 / awin_ref[HEAD_DIM:HEAD_DIM + 1, :]
    for r in range(HEADS_PER_GROUP):
        cols = slice(r * Q_BLOCK, (r + 1) * Q_BLOCK)
        o = (gates[3 * r:3 * r + 1] * ocmp_ref[:, cols]
             + gates[3 * r + 1:3 * r + 2] * inv_slc[:, cols] * aslc_ref[0:HEAD_DIM, cols]
             + gates[3 * r + 2:3 * r + 3] * inv_win[:, cols] * awin_ref[0:HEAD_DIM, cols])
        y_ref[0, :, r * HEAD_DIM:(r + 1) * HEAD_DIM] = o.T.astype(BF16)


def _nsa(qT, kc, kcT, ksl, vslT, kwin, vwinT, gT):
    b, _, s = qT.shape
    nb = kc.shape[2]
    ns = s // SEL_BLOCK
    g4 = HEADS_PER_GROUP * HEAD_DIM
    nq = HEADS_PER_GROUP * Q_BLOCK
    gates = gT.reshape(b, N_KV_GROUPS, HEADS_PER_GROUP * 3, s)
    per_bg = lambda bi, g, n: (bi, g, 0, 0)
    per_bg5 = lambda bi, g, n: (bi, g, 0, 0, 0)
    once = pl.Buffered(1)
    return pl.pallas_call(
        _nsa_kernel,
        grid=(b, N_KV_GROUPS, s // Q_BLOCK),
        in_specs=[
            pl.BlockSpec((1, g4, Q_BLOCK), lambda bi, g, n: (bi, g, n)),
            pl.BlockSpec((1, 1, nb, HEAD_DIM), per_bg, pipeline_mode=once),
            pl.BlockSpec((1, 1, HEAD_DIM, nb), lambda bi, g, n: (bi, N_KV_GROUPS + g, 0, 0), pipeline_mode=once),
            pl.BlockSpec((1, 1, s, 2 * HEAD_DIM), per_bg, pipeline_mode=once),
            pl.BlockSpec((1, 1, s // SLC_CHUNK, V_ROWS, SLC_CHUNK), per_bg5, pipeline_mode=once),
            pl.BlockSpec((1, 1, s, HEAD_DIM), per_bg, pipeline_mode=once),
            pl.BlockSpec((1, 1, s // WIN_CHUNK, V_ROWS, WIN_CHUNK), per_bg5, pipeline_mode=once),
            pl.BlockSpec((1, 1, HEADS_PER_GROUP * 3, Q_BLOCK), lambda bi, g, n: (bi, g, 0, n)),
        ],
        out_specs=pl.BlockSpec((1, Q_BLOCK, g4), lambda bi, g, n: (bi, n, g)),
        out_shape=jax.ShapeDtypeStruct((b, s, NSA_WIDTH), BF16),
        scratch_shapes=[
            pltpu.VMEM((HEAD_DIM, nq), BF16),
            pltpu.VMEM((nb + SUBLANES, Q_BLOCK), F32),
            pltpu.VMEM((ns, Q_BLOCK), F32),
            pltpu.VMEM((ns, Q_BLOCK), F32),
            pltpu.VMEM((pl.cdiv(ns, LANES) + 1, 2 * HEAD_DIM, nq), BF16),
            pltpu.VMEM((SLC_CHUNK, nq), F32),
            pltpu.VMEM((SLC_CHUNK, nq), F32),
            pltpu.VMEM((SLC_CHUNK, nq), F32),
            pltpu.VMEM((SUBLANES, nq), F32),
            pltpu.VMEM((WINDOW // WIN_CHUNK + 1, WIN_CHUNK, nq), F32),
            pltpu.VMEM((HEAD_DIM, nq), F32),
            pltpu.VMEM((V_ROWS, nq), F32),
            pltpu.VMEM((V_ROWS, nq), F32),
        ],
        compiler_params=_params("arbitrary", "arbitrary", "arbitrary"),
        name="nsa_attention",
    )(qT, kc, kcT, ksl, vslT, kwin, vwinT, gates)


def _outproj_kernel(yp_ref, yn_ref, wo_ref, x_ref, g1_ref, n2_ref, sc_ref, sh_ref, wrT_ref, br_ref, tri_ref,
                    x1_ref, h2_ref, idx_ref, wt_ref, rank_ref, cnt_ref, carry_ref):
    tm = x_ref.shape[1]
    n_exp = wrT_ref.shape[0]

    @pl.when((pl.program_id(0) == 0) & (pl.program_id(1) == 0))
    def _():
        carry_ref[...] = jnp.zeros(carry_ref.shape, F32)

    mix = _dot(yp_ref[0], wo_ref[0:POOL_WIDTH, :]) + _dot(yn_ref[0], wo_ref[POOL_WIDTH:POOL_WIDTH + NSA_WIDTH, :])
    x1 = x_ref[0] + g1_ref[0] * mix
    x1_ref[0] = x1
    h2 = _rmsnorm(x1, n2_ref[...]) * (1.0 + sc_ref[0]) + sh_ref[0]
    half = h2.shape[1] // 2
    h2_ref[0] = _pack_bf16_pair(h2[:, 0:half], h2[:, half:2 * half])

    logits = lax.dot_general(wrT_ref[...], h2.astype(BF16), (((1,), (1,)), ((), ())),
                             preferred_element_type=F32) + br_ref[...]
    ei = lax.broadcasted_iota(I32, (n_exp, tm), 0)
    vals, ids = [], []
    for _ in range(TOP_K):
        mx = jnp.max(logits, axis=0, keepdims=True)
        first = jnp.min(jnp.where(logits == mx, ei, n_exp), axis=0, keepdims=True)
        vals.append(mx)
        ids.append(first)
        logits = jnp.where(ei == first, -jnp.inf, logits)
    exps = [jnp.exp(v - vals[0]) for v in vals]
    den = exps[0]
    for e in exps[1:]:
        den = den + e

    onehot = jnp.zeros((n_exp, tm), F32)
    for first in ids:
        onehot = onehot + (ei == first).astype(F32)
    before = _dot(onehot.astype(BF16), tri_ref[...]) + carry_ref[:, 0:1]
    wt_ref[TOP_K:SUBLANES, :] = jnp.zeros((SUBLANES - TOP_K, tm), F32)
    for k in range(TOP_K):
        idx_ref[k:k + 1, :] = ids[k]
        wt_ref[k:k + 1, :] = exps[k] / den
        rank = jnp.sum(jnp.where(ei == ids[k], before, 0.0), axis=0, keepdims=True)
        rank_ref[k:k + 1, :] = rank.astype(I32)
    carry_ref[...] = carry_ref[...] + jnp.sum(onehot, axis=1, keepdims=True)
    cnt_ref[...] = carry_ref[...].astype(I32)


def _output_projection(y_pool, y_nsa, w_out, x, gate1, norm_g, scale, shift, w_router_t, b_router, tri):
    b, s, d = x.shape
    tm = PROJ_TM
    n_exp = w_router_t.shape[0]
    t_all = b * s
    tok = lambda bi, m: (bi, m, 0)
    per_b = lambda bi, m: (bi, 0, 0)
    const2 = lambda bi, m: (0, 0)
    flat = lambda bi, m: (0, bi * (s // tm) + m)
    return pl.pallas_call(
        _outproj_kernel,
        grid=(b, s // tm),
        in_specs=[
            pl.BlockSpec((1, tm, POOL_WIDTH), tok),
            pl.BlockSpec((1, tm, NSA_WIDTH), tok),
            pl.BlockSpec((POOL_WIDTH + NSA_WIDTH, d), const2, pipeline_mode=pl.Buffered(1)),
            pl.BlockSpec((1, tm, d), tok),
            pl.BlockSpec((1, 1, d), per_b),
            pl.BlockSpec((1, d), const2),
            pl.BlockSpec((1, 1, d), per_b),
            pl.BlockSpec((1, 1, d), per_b),
            pl.BlockSpec((n_exp, d), const2),
            pl.BlockSpec((n_exp, 1), const2),
            pl.BlockSpec((tm, tm), const2, pipeline_mode=pl.Buffered(1)),
        ],
        out_specs=(
            pl.BlockSpec((1, tm, d), tok),
            pl.BlockSpec((1, tm, d // 2), tok),
            pl.BlockSpec((TOP_K, tm), flat),
            pl.BlockSpec((SUBLANES, tm), flat),
            pl.BlockSpec((TOP_K, tm), flat),
            pl.BlockSpec((n_exp, LANES), const2),
        ),
        out_shape=(
            jax.ShapeDtypeStruct((b, s, d), F32),
            jax.ShapeDtypeStruct((b, s, d // 2), jnp.uint32),
            jax.ShapeDtypeStruct((TOP_K, t_all), I32),
            jax.ShapeDtypeStruct((SUBLANES, t_all), F32),
            jax.ShapeDtypeStruct((TOP_K, t_all), I32),
            jax.ShapeDtypeStruct((n_exp, LANES), I32),
        ),
        scratch_shapes=[pltpu.VMEM((n_exp, LANES), F32)],
        compiler_params=_params("arbitrary", "arbitrary"),
        name="output_projection_router",
    )(y_pool, y_nsa, w_out, x, gate1, norm_g, scale, shift, w_router_t, b_router, tri)


def _dest_kernel(ps_ref, idx_ref, rank_ref, dest_ref):
    idx = idx_ref[...]
    dest = rank_ref[...]
    for e in range(ps_ref.shape[0]):
        dest = dest + jnp.where(idx == e, ps_ref[e], 0)
    dest_ref[...] = dest


def _destinations(seg_starts, idx_t, rank_t):
    return pl.pallas_call(
        _dest_kernel,
        in_specs=[
            pl.BlockSpec(memory_space=pltpu.SMEM),
            pl.BlockSpec(memory_space=pltpu.VMEM),
            pl.BlockSpec(memory_space=pltpu.VMEM),
        ],
        out_specs=pl.BlockSpec(memory_space=pltpu.VMEM),
        out_shape=jax.ShapeDtypeStruct(idx_t.shape, I32),
        name="moe_destinations",
    )(seg_starts, idx_t, rank_t)


def _row_copy(src_ref, src_row, dst_ref, dst_row, sem):
    return pltpu.make_async_copy(src_ref.at[pl.ds(src_row, 1)], dst_ref.at[pl.ds(dst_row, 1)], sem)


def _dispatch_kernel(dest_ref, starts_ref, ends_ref, h_ref, xs_ref, zero_ref, sem, zsem):
    tm = h_ref.shape[0]
    n_exp = starts_ref.shape[0]
    zrows = zero_ref.shape[0]

    @pl.when(pl.program_id(0) == 0)
    def _():
        zero_ref[...] = jnp.zeros(zero_ref.shape, zero_ref.dtype)

        def clear_block(end_row, start):
            for r in range(EXPERT_TM // zrows):
                row0 = pl.multiple_of(end_row - EXPERT_TM + r * zrows, zrows)
                cp = pltpu.make_async_copy(zero_ref, xs_ref.at[pl.ds(row0, zrows)], zsem)
                cp.start() if start else cp.wait()

        def clear(e, start):
            @pl.when(ends_ref[e] > starts_ref[e])
            def _():
                clear_block(ends_ref[e], start)

        first_unused = ends_ref[n_exp - 1] // EXPERT_TM
        n_blk = xs_ref.shape[0] // EXPERT_TM
        for start in (True, False):
            lax.fori_loop(0, n_exp, lambda e, c: (clear(e, start), c)[1], 0)
            lax.fori_loop(first_unused, n_blk, lambda i, c: (clear_block((i + 1) * EXPERT_TM, start), c)[1], 0)

    def issue(t, carry):
        for k in range(TOP_K):
            _row_copy(h_ref, t, xs_ref, dest_ref[k, t], sem).start()
        return carry

    lax.fori_loop(0, tm, issue, 0)

    def drain(t, carry):
        for k in range(TOP_K):
            _row_copy(h_ref, t, xs_ref, dest_ref[k, t], sem).wait()
        return carry

    lax.fori_loop(0, tm, drain, 0)


def _dispatch(dest_t, seg_starts, seg_ends, h_rows, n_buf):
    t_all, d = h_rows.shape
    tm = DISPATCH_TM
    return pl.pallas_call(
        _dispatch_kernel,
        grid=(t_all // tm,),
        in_specs=[
            pl.BlockSpec((TOP_K, tm), lambda i: (0, i), memory_space=pltpu.SMEM),
            pl.BlockSpec(memory_space=pltpu.SMEM),
            pl.BlockSpec(memory_space=pltpu.SMEM),
            pl.BlockSpec((tm, d), lambda i: (i, 0)),
        ],
        out_specs=pl.BlockSpec(memory_space=pl.ANY),
        out_shape=jax.ShapeDtypeStruct((n_buf, d), h_rows.dtype),
        scratch_shapes=[pltpu.VMEM((DISPATCH_ZERO_ROWS, d), h_rows.dtype),
                        pltpu.SemaphoreType.DMA(()), pltpu.SemaphoreType.DMA(())],
        compiler_params=_params("arbitrary"),
        name="moe_dispatch",
    )(dest_t, seg_starts, seg_ends, h_rows)


def _ffn_up_kernel(be_ref, nu_ref, xs_ref, wg_ref, bg_ref, wu_ref, bu_ref, h_ref, xb_ref):
    del be_ref
    used = pl.program_id(0) < nu_ref[0]

    @pl.when(used & (pl.program_id(1) == 0))
    def _():
        half = xs_ref.shape[1]
        first, second = _unpack_bf16_pair(xs_ref[...])
        xb_ref[:, 0:half] = first.astype(BF16)
        xb_ref[:, half:2 * half] = second.astype(BF16)

    @pl.when(used)
    def _():
        xb = xb_ref[...]
        gl = jnp.minimum(_dot(xb, wg_ref[0, 0].astype(BF16)) + bg_ref[0, 0], SWIGLU_LIMIT)
        lin = jnp.clip(_dot(xb, wu_ref[0, 0].astype(BF16)) + bu_ref[0, 0], -SWIGLU_LIMIT, SWIGLU_LIMIT)
        h_ref[...] = (gl * jax.nn.sigmoid(SWIGLU_ALPHA * gl) * (lin + 1.0)).astype(BF16)

    @pl.when(jnp.logical_not(used))
    def _():
        h_ref[...] = jnp.zeros(h_ref.shape, BF16)


def _ffn_down_kernel(be_ref, nu_ref, h_ref, wd_ref, bd_ref, y_ref):
    del be_ref
    used = pl.program_id(1) < nu_ref[0]

    @pl.when(used)
    def _():
        y = _dot(h_ref[...], wd_ref[0, 0].astype(BF16)) + bd_ref[0, 0]
        half = y.shape[1] // 2
        y_ref[...] = _pack_bf16_pair(y[:, 0:half], y[:, half:2 * half])

    @pl.when(jnp.logical_not(used))
    def _():
        y_ref[...] = jnp.zeros(y_ref.shape, jnp.uint32)


def _expert_ffn(layer, block_expert, n_used, xs, w_gate, b_gate, w_up, b_up, w_down, b_down):
    n_buf = xs.shape[0]
    depth, n_exp, d, f = w_gate.shape
    tm, tf, td = EXPERT_TM, EXPERT_TF, EXPERT_TD
    n_blk = n_buf // tm

    nf = f // tf

    def up_rows(i, j, be, nu):
        return (jnp.minimum(i, nu[0] - 1), 0)

    def up_cols(i, j, be, nu):
        return (layer, be[i], 0, jnp.where(i < nu[0], j, nf - 1))

    def rows(j, i, be, nu):
        return (jnp.minimum(i, nu[0] - 1), 0)

    def cols(j, i, be, nu):
        return (layer, be[i], 0, j)

    def out_block(j, i, be, nu):
        return (i, j)

    hidden = pl.pallas_call(
        _ffn_up_kernel,
        grid_spec=pltpu.PrefetchScalarGridSpec(
            num_scalar_prefetch=2,
            grid=(n_blk, nf),
            in_specs=[
                pl.BlockSpec((tm, d // 2), up_rows),
                pl.BlockSpec((1, 1, d, tf), up_cols),
                pl.BlockSpec((1, 1, 1, tf), up_cols),
                pl.BlockSpec((1, 1, d, tf), up_cols),
                pl.BlockSpec((1, 1, 1, tf), up_cols),
            ],
            out_specs=pl.BlockSpec((tm, tf), lambda i, j, be, nu: (i, j)),
            scratch_shapes=[pltpu.VMEM((tm, d), BF16)],
        ),
        out_shape=jax.ShapeDtypeStruct((n_buf, f), BF16),
        compiler_params=_params("arbitrary", "arbitrary"),
        name="moe_ffn_up",
    )(block_expert, n_used, xs, w_gate, b_gate.reshape(depth, n_exp, 1, f), w_up, b_up.reshape(depth, n_exp, 1, f))
    return pl.pallas_call(
        _ffn_down_kernel,
        grid_spec=pltpu.PrefetchScalarGridSpec(
            num_scalar_prefetch=2,
            grid=(d // td, n_blk),
            in_specs=[
                pl.BlockSpec((tm, f), rows),
                pl.BlockSpec((1, 1, f, td), cols),
                pl.BlockSpec((1, 1, 1, td), cols),
            ],
            out_specs=pl.BlockSpec((tm, td // 2), out_block),
        ),
        out_shape=jax.ShapeDtypeStruct((n_buf, d // 2), jnp.uint32),
        compiler_params=_params("arbitrary", "arbitrary"),
        name="moe_ffn_down",
    )(block_expert, n_used, hidden, w_down, b_down.reshape(depth, n_exp, 1, d))


def _combine_kernel(dest_ref, wt_ref, x1_ref, g2_ref, fg_ref, y_ref, o_ref, rows_ref, sem, *, final_norm):
    tm = x1_ref.shape[1]

    def issue(t, carry):
        for k in range(TOP_K):
            _row_copy(y_ref, dest_ref[k, t], rows_ref.at[k], t, sem).start()
        return carry

    lax.fori_loop(0, tm, issue, 0)

    wt = jnp.concatenate([wt_ref[...], jnp.zeros((tm - SUBLANES, tm), F32)], axis=0).T

    def drain(t, carry):
        for k in range(TOP_K):
            _row_copy(y_ref, dest_ref[k, t], rows_ref.at[k], t, sem).wait()
        return carry

    lax.fori_loop(0, tm, drain, 0)

    acc_first, acc_second = None, None
    for k in range(TOP_K):
        first, second = _unpack_bf16_pair(rows_ref[k])
        w = wt[:, k:k + 1]
        acc_first = w * first if k == 0 else acc_first + w * first
        acc_second = w * second if k == 0 else acc_second + w * second
    half = EXPERT_TD // 2
    pieces = []
    for j in range(acc_first.shape[1] // half):
        pieces += [acc_first[:, j * half:(j + 1) * half], acc_second[:, j * half:(j + 1) * half]]
    out = x1_ref[0] + g2_ref[0] * jnp.concatenate(pieces, axis=1)
    if final_norm:
        out = _rmsnorm(out, fg_ref[...])
    o_ref[0] = out


def _combine(dest_t, wt_t, x1, gate2, final_g, y_rows, final_norm):
    b, s, d = x1.shape
    tm = COMBINE_TM
    flat = lambda bi, m: (0, bi * (s // tm) + m)
    return pl.pallas_call(
        functools.partial(_combine_kernel, final_norm=final_norm),
        grid=(b, s // tm),
        in_specs=[
            pl.BlockSpec((TOP_K, tm), flat, memory_space=pltpu.SMEM),
            pl.BlockSpec((SUBLANES, tm), flat),
            pl.BlockSpec((1, tm, d), lambda bi, m: (bi, m, 0)),
            pl.BlockSpec((1, 1, d), lambda bi, m: (bi, 0, 0)),
            pl.BlockSpec((1, d), lambda bi, m: (0, 0)),
            pl.BlockSpec(memory_space=pl.ANY),
        ],
        out_specs=pl.BlockSpec((1, tm, d), lambda bi, m: (bi, m, 0)),
        out_shape=jax.ShapeDtypeStruct((b, s, d), F32),
        scratch_shapes=[pltpu.VMEM((TOP_K, tm, d // 2), jnp.uint32), pltpu.SemaphoreType.DMA(())],
        compiler_params=_params("arbitrary", "arbitrary"),
        name="moe_combine",
    )(dest_t, wt_t, x1, gate2, final_g, y_rows)


def _segment_layout(counts, n_blk):
    padded = (counts + EXPERT_TM - 1) // EXPERT_TM * EXPERT_TM
    ends = jnp.cumsum(padded)
    starts = ends - padded
    n_used = ends[-1] // EXPERT_TM
    blk = jnp.minimum(jnp.arange(n_blk, dtype=I32), n_used - 1) * EXPERT_TM
    block_expert = jnp.minimum(jnp.searchsorted(ends, blk, side="right"), counts.shape[0] - 1)
    return starts.astype(I32), ends.astype(I32), block_expert.astype(I32), n_used.reshape(1).astype(I32)


def kernel(x, c, norm1_g, norm2_g, w_ada, b_ada, w_in, pool_w, pool_scale, cmp_pos_k, cmp_w1_k, cmp_b1_k, cmp_w2_k, cmp_b2_k, cmp_pos_v, cmp_w1_v, cmp_b1_v, cmp_w2_v, cmp_b2_v, w_out, w_router, b_router, w_gate, b_gate, w_up, b_up, w_down, b_down, final_g):
    b, s, d = x.shape
    depth = w_ada.shape[0]
    n_exp = w_router.shape[-1]
    t_all = b * s
    assert s % PROJ_TM == 0 and s % SLC_CHUNK == 0 and t_all % DISPATCH_TM == 0
    n_main = POOL_WIDTH + NSA_WIDTH + 6 * KV_WIDTH
    assert w_in.shape[-1] == n_main + N_GATE_COLS

    c_pad = jnp.pad(c, ((0, SUBLANES - b), (0, 0)))
    mod = _modulation(c_pad, w_ada, b_ada)[:, :b]
    tri = (jnp.arange(PROJ_TM)[:, None] < jnp.arange(PROJ_TM)[None, :]).astype(BF16)
    n_buf = (t_all * TOP_K + n_exp * (EXPERT_TM - 1)) // EXPERT_TM * EXPERT_TM
    n_blk = n_buf // EXPERT_TM

    for l in range(depth):
        sh1, sc1, g1, sh2, sc2, g2 = [m.reshape(b, 1, d) for m in jnp.split(mod[l], 6, axis=-1)]
        w_main = w_in[l, :, :n_main].astype(BF16)
        w_gates = jnp.pad(w_in[l, :, n_main:], ((0, 0), (0, LANES - N_GATE_COLS))).astype(BF16)
        u_pool, qT, kvc, ksl, vslT, kwin, vwinT, gT = _input_projection(
            x, norm1_g[l].reshape(1, d), sc1, sh1, w_main, w_gates)
        y_pool = _pool_mixer(u_pool, pool_w[l].astype(BF16), pool_scale[l].reshape(1, POOL_WIDTH))
        kc, kcT = _compress(
            kvc,
            jnp.stack([cmp_pos_k[l], cmp_pos_v[l]]).reshape(2, 1, CMP_BLOCK * HEAD_DIM),
            jnp.stack([cmp_w1_k[l], cmp_w1_v[l]]).astype(BF16),
            jnp.stack([cmp_b1_k[l], cmp_b1_v[l]])[:, None, :],
            jnp.stack([cmp_w2_k[l], cmp_w2_v[l]]).astype(BF16),
            jnp.stack([cmp_b2_k[l], cmp_b2_v[l]])[:, None, :])
        y_nsa = _nsa(qT, kc, kcT, ksl, vslT, kwin, vwinT, gT)
        x1, h2, idx_t, wt_t, rank_t, counts = _output_projection(
            y_pool, y_nsa, w_out[l].astype(BF16), x, g1, norm2_g[l].reshape(1, d), sc2, sh2,
            w_router[l].T.astype(BF16), b_router[l].reshape(n_exp, 1), tri)
        seg_starts, seg_ends, block_expert, n_used = _segment_layout(counts[:, 0], n_blk)
        dest_t = _destinations(seg_starts, idx_t, rank_t)
        xs = _dispatch(dest_t, seg_starts, seg_ends, h2.reshape(t_all, d // 2), n_buf)
        y_rows = _expert_ffn(l, block_expert, n_used, xs, w_gate, b_gate, w_up, b_up, w_down, b_down)
        x = _combine(dest_t, wt_t, x1, g2, final_g.reshape(1, d), y_rows, final_norm=(l == depth - 1))
    return x
```

```python
import functools

import jax
import jax.numpy as jnp
from jax import lax
from jax.experimental import pallas as pl
from jax.experimental.pallas import tpu as pltpu

F32 = jnp.float32
BF16 = jnp.bfloat16
I32 = jnp.int32

HEAD_DIM = 128
N_HEADS = 8
N_KV_GROUPS = 2
HEADS_PER_GROUP = N_HEADS // N_KV_GROUPS
POOL_WINDOWS = (2, 4, 8, 16)
POOL_GROUP = 256
POOL_WIDTH = POOL_GROUP * len(POOL_WINDOWS)
NSA_WIDTH = N_HEADS * HEAD_DIM
KV_WIDTH = N_KV_GROUPS * HEAD_DIM
N_GATE_COLS = 3 * N_HEADS
CMP_BLOCK = 32
CMP_STRIDE = 16
SEL_BLOCK = 64
SEL_RATIO = SEL_BLOCK // CMP_STRIDE
N_SELECT = 16
WINDOW = 512
Q_BLOCK = 128
TOP_K = 4
SWIGLU_ALPHA = 1.702
SWIGLU_LIMIT = 7.0
NORM_EPS = 1e-5
NEG = -1e30
BIG = 1e30
BELOW_NEG = -3e38
V_ROWS = HEAD_DIM + 16
LOG2_E = 1.4426950408889634

LANES = 128
SUBLANES = 8
VMEM_LIMIT_BYTES = 56 * 1024 * 1024

MOD_TN = 1024
PROJ_TM = 512
SLC_CHUNK = 256
SLC_UNROLL = 9
WIN_CHUNK = 128
DISPATCH_TM = 256
DISPATCH_ZERO_ROWS = 256
COMBINE_TM = 128
EXPERT_TM = 1024
EXPERT_TF = 512
EXPERT_TD = 1024


def _params(*sem):
    return pltpu.CompilerParams(dimension_semantics=sem, vmem_limit_bytes=VMEM_LIMIT_BYTES)


def _dot(a, b):
    return jnp.dot(a, b, preferred_element_type=F32)


def _rmsnorm(x, g):
    return x * lax.rsqrt(jnp.mean(x * x, axis=-1, keepdims=True) + NORM_EPS) * g


_HIGH_HALF = 0xFFFF0000


def _pack_bf16_pair(a, b):
    hi = pltpu.bitcast(a.astype(BF16).astype(F32), jnp.uint32) & jnp.uint32(_HIGH_HALF)
    lo = pltpu.bitcast(b.astype(BF16).astype(F32), jnp.uint32) >> 16
    return hi | lo


def _unpack_bf16_pair(w):
    return pltpu.bitcast(w & jnp.uint32(_HIGH_HALF), F32), pltpu.bitcast(w << 16, F32)


def _mod_kernel(c_ref, w_ref, b_ref, o_ref):
    c = c_ref[...]
    ca = (c * jax.nn.sigmoid(c)).astype(BF16)
    o_ref[0] = _dot(ca, w_ref[0].astype(BF16)) + b_ref[0]


def _modulation(c_pad, w_ada, b_ada):
    depth, d, n = w_ada.shape
    rows = c_pad.shape[0]
    return pl.pallas_call(
        _mod_kernel,
        grid=(depth, n // MOD_TN),
        in_specs=[
            pl.BlockSpec((rows, d), lambda l, j: (0, 0)),
            pl.BlockSpec((1, d, MOD_TN), lambda l, j: (l, 0, j)),
            pl.BlockSpec((1, 1, MOD_TN), lambda l, j: (l, 0, j)),
        ],
        out_specs=pl.BlockSpec((1, rows, MOD_TN), lambda l, j: (l, 0, j)),
        out_shape=jax.ShapeDtypeStruct((depth, rows, n), F32),
        compiler_params=_params("arbitrary", "arbitrary"),
        name="adaln_modulation",
    )(c_pad, w_ada, b_ada.reshape(depth, 1, n))


def _inproj_kernel(x_ref, g_ref, sc_ref, sh_ref, w_ref, wg_ref,
                   upool_ref, qT_ref, kvc_ref, ksl_ref, vslT_ref, kwin_ref, vwinT_ref, gT_ref):
    tm = x_ref.shape[1]
    h = _rmsnorm(x_ref[0], g_ref[...]) * (1.0 + sc_ref[0]) + sh_ref[0]
    hb = h.astype(BF16)
    upool_ref[0] = _dot(hb, w_ref[:, 0:POOL_WIDTH])
    q0 = POOL_WIDTH
    scale = HEAD_DIM ** -0.5 * LOG2_E
    def four_heads(col0):
        wide = _dot(hb, w_ref[:, col0:col0 + 4 * HEAD_DIM])
        return [wide[:, i * HEAD_DIM:(i + 1) * HEAD_DIM] for i in range(4)]

    for j in range(N_HEADS // 4):
        for i, q in enumerate(four_heads(q0 + j * 4 * HEAD_DIM)):
            hd = 4 * j + i
            qT_ref[0, hd * HEAD_DIM:(hd + 1) * HEAD_DIM, :] = (q * scale).T.astype(BF16)
    c0 = q0 + NSA_WIDTH
    for i, kv in enumerate(four_heads(c0)):
        kvc_ref[0, i] = kv
    s0 = c0 + 2 * KV_WIDTH
    w0 = s0 + 2 * KV_WIDTH
    slc = four_heads(s0)
    win = four_heads(w0)
    tpos = pl.program_id(1) * tm + lax.broadcasted_iota(I32, (tm, HEAD_DIM), 0)
    lane = lax.broadcasted_iota(I32, (tm, HEAD_DIM), 1)
    block_onehot = jnp.where(lane == ((tpos // SEL_BLOCK) & (LANES - 1)), 1.0, 0.0).astype(BF16)
    for g in range(N_KV_GROUPS):
        ksl_ref[0, g, :, 0:HEAD_DIM] = slc[g].astype(BF16)
        ksl_ref[0, g, :, HEAD_DIM:2 * HEAD_DIM] = block_onehot
        vT = slc[N_KV_GROUPS + g].T.astype(BF16)
        for c in range(tm // SLC_CHUNK):
            vslT_ref[0, g, c, 0:HEAD_DIM, :] = vT[:, c * SLC_CHUNK:(c + 1) * SLC_CHUNK]
            vslT_ref[0, g, c, HEAD_DIM:V_ROWS, :] = jnp.ones((V_ROWS - HEAD_DIM, SLC_CHUNK), BF16)
        kwin_ref[0, g] = win[g].astype(BF16)
        vT = win[N_KV_GROUPS + g].T.astype(BF16)
        for c in range(tm // WIN_CHUNK):
            vwinT_ref[0, g, c, 0:HEAD_DIM, :] = vT[:, c * WIN_CHUNK:(c + 1) * WIN_CHUNK]
            vwinT_ref[0, g, c, HEAD_DIM:V_ROWS, :] = jnp.ones((V_ROWS - HEAD_DIM, WIN_CHUNK), BF16)
    gl = jax.nn.sigmoid(_dot(hb, wg_ref[...]))
    gT_ref[0] = gl.T[0:N_GATE_COLS, :]


def _input_projection(x, norm_g, scale, shift, w_main, w_gates):
    b, s, d = x.shape
    tm = PROJ_TM
    n_main = w_main.shape[1]
    tok = lambda bi, m: (bi, m, 0)
    per_b = lambda bi, m: (bi, 0, 0)
    const2 = lambda bi, m: (0, 0)
    out_shape = (
        jax.ShapeDtypeStruct((b, s, POOL_WIDTH), F32),
        jax.ShapeDtypeStruct((b, NSA_WIDTH, s), BF16),
        jax.ShapeDtypeStruct((b, 2 * N_KV_GROUPS, s, HEAD_DIM), F32),
        jax.ShapeDtypeStruct((b, N_KV_GROUPS, s, 2 * HEAD_DIM), BF16),
        jax.ShapeDtypeStruct((b, N_KV_GROUPS, s // SLC_CHUNK, V_ROWS, SLC_CHUNK), BF16),
        jax.ShapeDtypeStruct((b, N_KV_GROUPS, s, HEAD_DIM), BF16),
        jax.ShapeDtypeStruct((b, N_KV_GROUPS, s // WIN_CHUNK, V_ROWS, WIN_CHUNK), BF16),
        jax.ShapeDtypeStruct((b, N_GATE_COLS, s), F32),
    )
    out_specs = (
        pl.BlockSpec((1, tm, POOL_WIDTH), tok),
        pl.BlockSpec((1, NSA_WIDTH, tm), lambda bi, m: (bi, 0, m)),
        pl.BlockSpec((1, 2 * N_KV_GROUPS, tm, HEAD_DIM), lambda bi, m: (bi, 0, m, 0)),
        pl.BlockSpec((1, N_KV_GROUPS, tm, 2 * HEAD_DIM), lambda bi, m: (bi, 0, m, 0)),
        pl.BlockSpec((1, N_KV_GROUPS, tm // SLC_CHUNK, V_ROWS, SLC_CHUNK), lambda bi, m: (bi, 0, m, 0, 0)),
        pl.BlockSpec((1, N_KV_GROUPS, tm, HEAD_DIM), lambda bi, m: (bi, 0, m, 0)),
        pl.BlockSpec((1, N_KV_GROUPS, tm // WIN_CHUNK, V_ROWS, WIN_CHUNK), lambda bi, m: (bi, 0, m, 0, 0)),
        pl.BlockSpec((1, N_GATE_COLS, tm), lambda bi, m: (bi, 0, m)),
    )
    return pl.pallas_call(
        _inproj_kernel,
        grid=(b, s // tm),
        in_specs=[
            pl.BlockSpec((1, tm, d), tok),
            pl.BlockSpec((1, d), const2),
            pl.BlockSpec((1, 1, d), per_b),
            pl.BlockSpec((1, 1, d), per_b),
            pl.BlockSpec((d, n_main), const2, pipeline_mode=pl.Buffered(1)),
            pl.BlockSpec((d, LANES), const2, pipeline_mode=pl.Buffered(1)),
        ],
        out_specs=out_specs,
        out_shape=out_shape,
        compiler_params=_params("arbitrary", "arbitrary"),
        name="input_projection",
    )(x, norm_g, scale, shift, w_main, w_gates)


def _pool_kernel(u_ref, pw_ref, ps_ref, o_ref, ext_ref):
    tm = u_ref.shape[1]
    halo = max(POOL_WINDOWS)
    m = pl.program_id(1)

    @pl.when(m == 0)
    def _():
        ext_ref[0:halo, :] = jnp.zeros((halo, POOL_WIDTH), F32)

    ext_ref[halo:halo + tm, :] = u_ref[0]
    t = m * tm + lax.broadcasted_iota(I32, (tm, 1), 0)
    for gi, w in enumerate(POOL_WINDOWS):
        cols = slice(gi * POOL_GROUP, (gi + 1) * POOL_GROUP)
        ug = ext_ref[halo:halo + tm, cols]
        acc = ug
        for i in range(1, w):
            acc = acc + ext_ref[halo - i:halo - i + tm, cols]
        cnt = jnp.minimum(t + 1, w).astype(F32)
        diff = (acc / cnt - ug).astype(BF16)
        o_ref[0, :, cols] = (_dot(diff, pw_ref[gi]) * ps_ref[:, cols]).astype(BF16)
    ext_ref[0:halo, :] = ext_ref[tm:tm + halo, :]


def _pool_mixer(u_pool, pool_w, pool_scale):
    b, s, _ = u_pool.shape
    tm = PROJ_TM
    return pl.pallas_call(
        _pool_kernel,
        grid=(b, s // tm),
        in_specs=[
            pl.BlockSpec((1, tm, POOL_WIDTH), lambda bi, m: (bi, m, 0)),
            pl.BlockSpec((len(POOL_WINDOWS), POOL_GROUP, POOL_GROUP), lambda bi, m: (0, 0, 0)),
            pl.BlockSpec((1, POOL_WIDTH), lambda bi, m: (0, 0)),
        ],
        out_specs=pl.BlockSpec((1, tm, POOL_WIDTH), lambda bi, m: (bi, m, 0)),
        out_shape=jax.ShapeDtypeStruct((b, s, POOL_WIDTH), BF16),
        scratch_shapes=[pltpu.VMEM((tm + max(POOL_WINDOWS), POOL_WIDTH), F32)],
        compiler_params=_params("arbitrary", "arbitrary"),
        name="pool_mixer",
    )(u_pool, pool_w, pool_scale)


def _compress_kernel(a_ref, pos_ref, w1_ref, b1_ref, w2_ref, b2_ref, o_ref, oT_ref):
    half = CMP_STRIDE * HEAD_DIM
    a = a_ref[0, 0]
    nb = a.shape[0]
    pos = pos_ref[0]
    a_lo = (a + pos[:, 0:half]).astype(BF16)
    a_hi = (a + pos[:, half:2 * half]).astype(BF16)
    p_lo = _dot(a_lo, w1_ref[0, 0:half, :])
    p_hi = _dot(a_hi, w1_ref[0, half:2 * half, :])
    hid = jax.nn.gelu(p_lo + pltpu.roll(p_hi, nb - 1, 0) + b1_ref[0])
    out = _dot(hid.astype(BF16), w2_ref[0]) + b2_ref[0]
    o_ref[0, 0] = out.astype(BF16)
    oT_ref[0, 0] = out.T.astype(BF16)


def _compress(kvc, pos, w1, b1, w2, b2):
    b, n4, s, _ = kvc.shape
    nb = s // CMP_STRIDE
    a = kvc.reshape(b, n4, nb, CMP_STRIDE * HEAD_DIM)
    hidden = w1.shape[-1]
    kv = lambda bi, i: (i // N_KV_GROUPS, 0, 0)
    return pl.pallas_call(
        _compress_kernel,
        grid=(b, n4),
        in_specs=[
            pl.BlockSpec((1, 1, nb, CMP_STRIDE * HEAD_DIM), lambda bi, i: (bi, i, 0, 0)),
            pl.BlockSpec((1, 1, CMP_BLOCK * HEAD_DIM), kv),
            pl.BlockSpec((1, CMP_BLOCK * HEAD_DIM, hidden), kv),
            pl.BlockSpec((1, 1, hidden), kv),
            pl.BlockSpec((1, hidden, HEAD_DIM), kv),
            pl.BlockSpec((1, 1, HEAD_DIM), kv),
        ],
        out_specs=(
            pl.BlockSpec((1, 1, nb, HEAD_DIM), lambda bi, i: (bi, i, 0, 0)),
            pl.BlockSpec((1, 1, HEAD_DIM, nb), lambda bi, i: (bi, i, 0, 0)),
        ),
        out_shape=(
            jax.ShapeDtypeStruct((b, n4, nb, HEAD_DIM), BF16),
            jax.ShapeDtypeStruct((b, n4, HEAD_DIM, nb), BF16),
        ),
        compiler_params=_params("arbitrary", "arbitrary"),
        name="kv_compress",
    )(a, pos, w1, b1, w2, b2)


def _flash_step(s, v_t, m_old, acc_ref):
    m_new = jnp.maximum(m_old, jnp.max(s, axis=0, keepdims=True))
    alpha = jnp.exp2(m_old - m_new)
    p = jnp.exp2(s - m_new)
    acc_ref[...] = acc_ref[...] * alpha + _dot(v_t, p.astype(BF16))
    return m_new


def _nsa_kernel(qT_ref, kc_ref, vcT_ref, ksl_ref, vslT_ref, kwin_ref, vwinT_ref, gT_ref, y_ref,
                qcat_ref, pbuf_ref, score_ref, sel_ref, wq_ref, sa_ref, sb_ref, sc_ref, ml_ref, wbuf_ref,
                ocmp_ref, aslc_ref, awin_ref):
    n = pl.program_id(2)
    q_start = n * Q_BLOCK
    nq = HEADS_PER_GROUP * Q_BLOCK
    nb = kc_ref.shape[2]
    ns = sel_ref.shape[0]

    for r in range(HEADS_PER_GROUP):
        qcat_ref[:, r * Q_BLOCK:(r + 1) * Q_BLOCK] = qT_ref[0, r * HEAD_DIM:(r + 1) * HEAD_DIM, :]
    qc = qcat_ref[...]
    tin = lax.broadcasted_iota(I32, (1, nq), 1) & (Q_BLOCK - 1)
    tok = q_start + tin

    sel_ref[...] = jnp.full((ns, Q_BLOCK), NEG, F32)
    t1 = q_start + lax.broadcasted_iota(I32, (1, Q_BLOCK), 1)
    tj = t1 // SEL_BLOCK

    def compress_and_select(rows):
        srows = rows // SEL_RATIO
        s = _dot(kc_ref[0, 0, 0:rows, :], qc)
        ci = lax.broadcasted_iota(I32, (rows, nq), 0)
        s = jnp.where(ci * CMP_STRIDE + (CMP_BLOCK - 1) <= tok, s, NEG)
        m = jnp.max(s, axis=0, keepdims=True)
        e = jnp.exp2(s - m)
        l = jnp.sum(e, axis=0, keepdims=True)
        inv = jnp.where(tok >= CMP_BLOCK - 1, 1.0 / l, 0.0)
        p = e * inv
        ocmp_ref[...] = _dot(vcT_ref[0, 0, :, 0:rows], p.astype(BF16))

        psum = p[:, 0:Q_BLOCK]
        for r in range(1, HEADS_PER_GROUP):
            psum = psum + p[:, r * Q_BLOCK:(r + 1) * Q_BLOCK]
        pbuf_ref[0:SUBLANES, :] = jnp.zeros((SUBLANES, Q_BLOCK), F32)
        pbuf_ref[SUBLANES:SUBLANES + rows, :] = psum
        imp = pbuf_ref[pl.ds(SUBLANES - 1, srows, stride=SEL_RATIO), :]
        for d in range(SEL_RATIO):
            imp = imp + pbuf_ref[pl.ds(SUBLANES + d, srows, stride=SEL_RATIO), :]

        ji = lax.broadcasted_iota(I32, (srows, Q_BLOCK), 0)
        forced = (ji == 0) | (ji == tj) | (ji == tj - 1)
        sel_ref[0:srows, :] = jnp.where(forced, 0.0, NEG)
        score_ref[0:srows, :] = jnp.where(forced, BELOW_NEG, jnp.where(ji * SEL_BLOCK <= t1, imp, NEG))
        n_forced = 1 + (tj > 0).astype(I32) + (tj > 1).astype(I32)
        quota = min(N_SELECT, srows) - n_forced

        def pick(it, carry):
            sc = score_ref[0:srows, :]
            mx = jnp.max(sc, axis=0, keepdims=True)
            first = jnp.min(jnp.where(sc == mx, ji, srows), axis=0, keepdims=True)
            hit = ji == first
            sel_ref[0:srows, :] = jnp.where(hit & (mx >= 0.0) & (it < quota), 0.0, sel_ref[0:srows, :])
            score_ref[0:srows, :] = jnp.where(hit, BELOW_NEG, sc)
            return carry

        n_picks = min(N_SELECT, srows) - jnp.where(n == 0, 1, 3)
        lax.fori_loop(0, n_picks, pick, 0)

    per_q = Q_BLOCK // CMP_STRIDE
    classes = sorted({r for r in (nb // 4, nb // 2) if r % (SEL_RATIO * SUBLANES * 2) == 0} | {nb})
    for k, rows in enumerate(classes):
        lo = 0 if k == 0 else classes[k - 1] // per_q
        in_class = (n >= lo) if k == len(classes) - 1 else ((n >= lo) & (n < rows // per_q))
        pl.when(in_class)(functools.partial(compress_and_select, rows))

    sel = sel_ref[...]
    wq_ref[wq_ref.shape[0] - 1, 0:HEAD_DIM, :] = jnp.zeros((HEAD_DIM, nq), BF16)
    wq_ref[wq_ref.shape[0] - 1, HEAD_DIM:HEAD_DIM + LANES, :] = jnp.full((LANES, nq), NEG, BF16)
    for h in range(wq_ref.shape[0] - 1):
        rows_h = min(LANES, ns - LANES * h)
        wq_ref[h, 0:HEAD_DIM, :] = qc
        blk = sel[LANES * h:LANES * h + rows_h].astype(BF16)
        for r in range(HEADS_PER_GROUP):
            wq_ref[h, HEAD_DIM:HEAD_DIM + rows_h, r * Q_BLOCK:(r + 1) * Q_BLOCK] = blk
        if rows_h < LANES:
            wq_ref[h, HEAD_DIM + rows_h:HEAD_DIM + LANES, :] = jnp.zeros((LANES - rows_h, nq), BF16)

    m0 = jnp.full((1, nq), NEG, F32)

    aslc_ref[...] = jnp.zeros(aslc_ref.shape, F32)
    blocks_per_chunk = SLC_CHUNK // SEL_BLOCK

    ml_ref[0:1, :] = m0

    n_past = q_start // SLC_CHUNK
    last_chunk = ksl_ref.shape[2] // SLC_CHUNK - 1
    n_half = wq_ref.shape[0] - 1

    def slc_scores(c, buf_ref):
        slot = jnp.where(c < n_past, (c * blocks_per_chunk) // LANES, n_half)
        k0 = pl.multiple_of(jnp.minimum(c, last_chunk) * SLC_CHUNK, SLC_CHUNK)
        buf_ref[...] = _dot(ksl_ref[0, 0, pl.ds(k0, SLC_CHUNK), :], wq_ref[slot])

    def slc_flash(c, buf_ref):
        v_t = vslT_ref[0, 0, jnp.minimum(c, last_chunk)]
        ml_ref[0:1, :] = _flash_step(buf_ref[...], v_t, ml_ref[0:1, :], aslc_ref)

    def slc_round(i, carry):
        bufs = (sa_ref, sb_ref, sc_ref)
        for k in range(SLC_UNROLL):
            slc_scores(SLC_UNROLL * i + k + 2, bufs[(k + 2) % 3])
            slc_flash(SLC_UNROLL * i + k, bufs[k % 3])
        return carry

    slc_scores(0, sa_ref)
    slc_scores(1, sb_ref)
    lax.fori_loop(0, (n_past + SLC_UNROLL - 1) // SLC_UNROLL, slc_round, 0)

    k0 = pl.multiple_of(n_past * SLC_CHUNK, SLC_CHUNK)
    sd = _dot(ksl_ref[0, 0, pl.ds(k0, SLC_CHUNK), :], wq_ref[(n_past * blocks_per_chunk) // LANES])
    kpos = k0 + lax.broadcasted_iota(I32, (SLC_CHUNK, nq), 0)
    sd = jnp.where(kpos <= tok, sd, NEG)
    _flash_step(sd, vslT_ref[0, 0, n_past], ml_ref[0:1, :], aslc_ref)

    n_wc = WINDOW // WIN_CHUNK + 1
    kk = lax.broadcasted_iota(I32, (WIN_CHUNK, nq), 0)
    chunk_ids = []
    m_win = m0
    for i in range(n_wc):
        wb = n - (n_wc - 1) + i
        wbc = jnp.maximum(wb, 0)
        chunk_ids.append(wbc)
        k0 = pl.multiple_of(wbc * WIN_CHUNK, WIN_CHUNK)
        sc = _dot(kwin_ref[0, 0, pl.ds(k0, WIN_CHUNK), :], qc)
        if i == 0:
            sc = jnp.where(kk > tin, sc, NEG)
        if i == n_wc - 1:
            sc = jnp.where(kk <= tin, sc, NEG)
        else:
            sc = sc + jnp.where(wb >= 0, 0.0, NEG)
        wbuf_ref[i] = sc
        m_win = jnp.maximum(m_win, jnp.max(sc, axis=0, keepdims=True))
    for i in range(n_wc):
        p = jnp.exp2(wbuf_ref[i] - m_win)
        pv = _dot(vwinT_ref[0, 0, chunk_ids[i]], p.astype(BF16))
        if i == 0:
            awin_ref[...] = pv
        else:
            awin_ref[...] = awin_ref[...] + pv

    gates = gT_ref[0, 0]
    inv_slc = 1.0 / aslc_ref[HEAD_DIM:HEAD_DIM + 1, :]
    inv_win = 1.0 / awin_ref[HEAD_DIM:HEAD_DIM + 1, :]
    for r in range(HEADS_PER_GROUP):
        cols = slice(r * Q_BLOCK, (r + 1) * Q_BLOCK)
        o = (gates[3 * r:3 * r + 1] * ocmp_ref[:, cols]
             + gates[3 * r + 1:3 * r + 2] * inv_slc[:, cols] * aslc_ref[0:HEAD_DIM, cols]
             + gates[3 * r + 2:3 * r + 3] * inv_win[:, cols] * awin_ref[0:HEAD_DIM, cols])
        y_ref[0, :, r * HEAD_DIM:(r + 1) * HEAD_DIM] = o.T.astype(BF16)


def _nsa(qT, kc, kcT, ksl, vslT, kwin, vwinT, gT):
    b, _, s = qT.shape
    nb = kc.shape[2]
    ns = s // SEL_BLOCK
    g4 = HEADS_PER_GROUP * HEAD_DIM
    nq = HEADS_PER_GROUP * Q_BLOCK
    gates = gT.reshape(b, N_KV_GROUPS, HEADS_PER_GROUP * 3, s)
    per_bg = lambda bi, g, n: (bi, g, 0, 0)
    per_bg5 = lambda bi, g, n: (bi, g, 0, 0, 0)
    once = pl.Buffered(1)
    return pl.pallas_call(
        _nsa_kernel,
        grid=(b, N_KV_GROUPS, s // Q_BLOCK),
        in_specs=[
            pl.BlockSpec((1, g4, Q_BLOCK), lambda bi, g, n: (bi, g, n)),
            pl.BlockSpec((1, 1, nb, HEAD_DIM), per_bg, pipeline_mode=once),
            pl.BlockSpec((1, 1, HEAD_DIM, nb), lambda bi, g, n: (bi, N_KV_GROUPS + g, 0, 0), pipeline_mode=once),
            pl.BlockSpec((1, 1, s, 2 * HEAD_DIM), per_bg, pipeline_mode=once),
            pl.BlockSpec((1, 1, s // SLC_CHUNK, V_ROWS, SLC_CHUNK), per_bg5, pipeline_mode=once),
            pl.BlockSpec((1, 1, s, HEAD_DIM), per_bg, pipeline_mode=once),
            pl.BlockSpec((1, 1, s // WIN_CHUNK, V_ROWS, WIN_CHUNK), per_bg5, pipeline_mode=once),
            pl.BlockSpec((1, 1, HEADS_PER_GROUP * 3, Q_BLOCK), lambda bi, g, n: (bi, g, 0, n)),
        ],
        out_specs=pl.BlockSpec((1, Q_BLOCK, g4), lambda bi, g, n: (bi, n, g)),
        out_shape=jax.ShapeDtypeStruct((b, s, NSA_WIDTH), BF16),
        scratch_shapes=[
            pltpu.VMEM((HEAD_DIM, nq), BF16),
            pltpu.VMEM((nb + SUBLANES, Q_BLOCK), F32),
            pltpu.VMEM((ns, Q_BLOCK), F32),
            pltpu.VMEM((ns, Q_BLOCK), F32),
            pltpu.VMEM((pl.cdiv(ns, LANES) + 1, 2 * HEAD_DIM, nq), BF16),
            pltpu.VMEM((SLC_CHUNK, nq), F32),
            pltpu.VMEM((SLC_CHUNK, nq), F32),
            pltpu.VMEM((SLC_CHUNK, nq), F32),
            pltpu.VMEM((SUBLANES, nq), F32),
            pltpu.VMEM((WINDOW // WIN_CHUNK + 1, WIN_CHUNK, nq), F32),
            pltpu.VMEM((HEAD_DIM, nq), F32),
            pltpu.VMEM((V_ROWS, nq), F32),
            pltpu.VMEM((V_ROWS, nq), F32),
        ],
        compiler_params=_params("arbitrary", "arbitrary", "arbitrary"),
        name="nsa_attention",
    )(qT, kc, kcT, ksl, vslT, kwin, vwinT, gates)


def _outproj_kernel(yp_ref, yn_ref, wo_ref, x_ref, g1_ref, n2_ref, sc_ref, sh_ref, wrT_ref, br_ref, tri_ref,
                    x1_ref, h2_ref, idx_ref, wt_ref, rank_ref, cnt_ref, carry_ref):
    tm = x_ref.shape[1]
    n_exp = wrT_ref.shape[0]

    @pl.when((pl.program_id(0) == 0) & (pl.program_id(1) == 0))
    def _():
        carry_ref[...] = jnp.zeros(carry_ref.shape, F32)

    mix = _dot(yp_ref[0], wo_ref[0:POOL_WIDTH, :]) + _dot(yn_ref[0], wo_ref[POOL_WIDTH:POOL_WIDTH + NSA_WIDTH, :])
    x1 = x_ref[0] + g1_ref[0] * mix
    x1_ref[0] = x1
    h2 = _rmsnorm(x1, n2_ref[...]) * (1.0 + sc_ref[0]) + sh_ref[0]
    half = h2.shape[1] // 2
    h2_ref[0] = _pack_bf16_pair(h2[:, 0:half], h2[:, half:2 * half])

    logits = lax.dot_general(wrT_ref[...], h2.astype(BF16), (((1,), (1,)), ((), ())),
                             preferred_element_type=F32) + br_ref[...]
    ei = lax.broadcasted_iota(I32, (n_exp, tm), 0)
    vals, ids = [], []
    for _ in range(TOP_K):
        mx = jnp.max(logits, axis=0, keepdims=True)
        first = jnp.min(jnp.where(logits == mx, ei, n_exp), axis=0, keepdims=True)
        vals.append(mx)
        ids.append(first)
        logits = jnp.where(ei == first, -jnp.inf, logits)
    exps = [jnp.exp(v - vals[0]) for v in vals]
    den = exps[0]
    for e in exps[1:]:
        den = den + e

    onehot = jnp.zeros((n_exp, tm), F32)
    for first in ids:
        onehot = onehot + (ei == first).astype(F32)
    before = _dot(onehot.astype(BF16), tri_ref[...]) + carry_ref[:, 0:1]
    wt_ref[TOP_K:SUBLANES, :] = jnp.zeros((SUBLANES - TOP_K, tm), F32)
    for k in range(TOP_K):
        idx_ref[k:k + 1, :] = ids[k]
        wt_ref[k:k + 1, :] = exps[k] / den
        rank = jnp.sum(jnp.where(ei == ids[k], before, 0.0), axis=0, keepdims=True)
        rank_ref[k:k + 1, :] = rank.astype(I32)
    carry_ref[...] = carry_ref[...] + jnp.sum(onehot, axis=1, keepdims=True)
    cnt_ref[...] = carry_ref[...].astype(I32)


def _output_projection(y_pool, y_nsa, w_out, x, gate1, norm_g, scale, shift, w_router_t, b_router, tri):
    b, s, d = x.shape
    tm = PROJ_TM
    n_exp = w_router_t.shape[0]
    t_all = b * s
    tok = lambda bi, m: (bi, m, 0)
    per_b = lambda bi, m: (bi, 0, 0)
    const2 = lambda bi, m: (0, 0)
    flat = lambda bi, m: (0, bi * (s // tm) + m)
    return pl.pallas_call(
        _outproj_kernel,
        grid=(b, s // tm),
        in_specs=[
            pl.BlockSpec((1, tm, POOL_WIDTH), tok),
            pl.BlockSpec((1, tm, NSA_WIDTH), tok),
            pl.BlockSpec((POOL_WIDTH + NSA_WIDTH, d), const2, pipeline_mode=pl.Buffered(1)),
            pl.BlockSpec((1, tm, d), tok),
            pl.BlockSpec((1, 1, d), per_b),
            pl.BlockSpec((1, d), const2),
            pl.BlockSpec((1, 1, d), per_b),
            pl.BlockSpec((1, 1, d), per_b),
            pl.BlockSpec((n_exp, d), const2),
            pl.BlockSpec((n_exp, 1), const2),
            pl.BlockSpec((tm, tm), const2, pipeline_mode=pl.Buffered(1)),
        ],
        out_specs=(
            pl.BlockSpec((1, tm, d), tok),
            pl.BlockSpec((1, tm, d // 2), tok),
            pl.BlockSpec((TOP_K, tm), flat),
            pl.BlockSpec((SUBLANES, tm), flat),
            pl.BlockSpec((TOP_K, tm), flat),
            pl.BlockSpec((n_exp, LANES), const2),
        ),
        out_shape=(
            jax.ShapeDtypeStruct((b, s, d), F32),
            jax.ShapeDtypeStruct((b, s, d // 2), jnp.uint32),
            jax.ShapeDtypeStruct((TOP_K, t_all), I32),
            jax.ShapeDtypeStruct((SUBLANES, t_all), F32),
            jax.ShapeDtypeStruct((TOP_K, t_all), I32),
            jax.ShapeDtypeStruct((n_exp, LANES), I32),
        ),
        scratch_shapes=[pltpu.VMEM((n_exp, LANES), F32)],
        compiler_params=_params("arbitrary", "arbitrary"),
        name="output_projection_router",
    )(y_pool, y_nsa, w_out, x, gate1, norm_g, scale, shift, w_router_t, b_router, tri)


def _dest_kernel(ps_ref, idx_ref, rank_ref, dest_ref):
    idx = idx_ref[...]
    dest = rank_ref[...]
    for e in range(ps_ref.shape[0]):
        dest = dest + jnp.where(idx == e, ps_ref[e], 0)
    dest_ref[...] = dest


def _destinations(seg_starts, idx_t, rank_t):
    return pl.pallas_call(
        _dest_kernel,
        in_specs=[
            pl.BlockSpec(memory_space=pltpu.SMEM),
            pl.BlockSpec(memory_space=pltpu.VMEM),
            pl.BlockSpec(memory_space=pltpu.VMEM),
        ],
        out_specs=pl.BlockSpec(memory_space=pltpu.VMEM),
        out_shape=jax.ShapeDtypeStruct(idx_t.shape, I32),
        name="moe_destinations",
    )(seg_starts, idx_t, rank_t)


def _row_copy(src_ref, src_row, dst_ref, dst_row, sem):
    return pltpu.make_async_copy(src_ref.at[pl.ds(src_row, 1)], dst_ref.at[pl.ds(dst_row, 1)], sem)


def _dispatch_kernel(dest_ref, starts_ref, ends_ref, h_ref, xs_ref, zero_ref, sem, zsem):
    tm = h_ref.shape[0]
    n_exp = starts_ref.shape[0]
    zrows = zero_ref.shape[0]

    @pl.when(pl.program_id(0) == 0)
    def _():
        zero_ref[...] = jnp.zeros(zero_ref.shape, zero_ref.dtype)

        def clear_block(end_row, start):
            for r in range(EXPERT_TM // zrows):
                row0 = pl.multiple_of(end_row - EXPERT_TM + r * zrows, zrows)
                cp = pltpu.make_async_copy(zero_ref, xs_ref.at[pl.ds(row0, zrows)], zsem)
                cp.start() if start else cp.wait()

        def clear(e, start):
            @pl.when(ends_ref[e] > starts_ref[e])
            def _():
                clear_block(ends_ref[e], start)

        first_unused = ends_ref[n_exp - 1] // EXPERT_TM
        n_blk = xs_ref.shape[0] // EXPERT_TM
        for start in (True, False):
            lax.fori_loop(0, n_exp, lambda e, c: (clear(e, start), c)[1], 0)
            lax.fori_loop(first_unused, n_blk, lambda i, c: (clear_block((i + 1) * EXPERT_TM, start), c)[1], 0)

    def issue(t, carry):
        for k in range(TOP_K):
            _row_copy(h_ref, t, xs_ref, dest_ref[k, t], sem).start()
        return carry

    lax.fori_loop(0, tm, issue, 0)

    def drain(t, carry):
        for k in range(TOP_K):
            _row_copy(h_ref, t, xs_ref, dest_ref[k, t], sem).wait()
        return carry

    lax.fori_loop(0, tm, drain, 0)


def _dispatch(dest_t, seg_starts, seg_ends, h_rows, n_buf):
    t_all, d = h_rows.shape
    tm = DISPATCH_TM
    return pl.pallas_call(
        _dispatch_kernel,
        grid=(t_all // tm,),
        in_specs=[
            pl.BlockSpec((TOP_K, tm), lambda i: (0, i), memory_space=pltpu.SMEM),
            pl.BlockSpec(memory_space=pltpu.SMEM),
            pl.BlockSpec(memory_space=pltpu.SMEM),
            pl.BlockSpec((tm, d), lambda i: (i, 0)),
        ],
        out_specs=pl.BlockSpec(memory_space=pl.ANY),
        out_shape=jax.ShapeDtypeStruct((n_buf, d), h_rows.dtype),
        scratch_shapes=[pltpu.VMEM((DISPATCH_ZERO_ROWS, d), h_rows.dtype),
                        pltpu.SemaphoreType.DMA(()), pltpu.SemaphoreType.DMA(())],
        compiler_params=_params("arbitrary"),
        name="moe_dispatch",
    )(dest_t, seg_starts, seg_ends, h_rows)


def _ffn_up_kernel(be_ref, nu_ref, xs_ref, wg_ref, bg_ref, wu_ref, bu_ref, h_ref, xb_ref):
    del be_ref
    used = pl.program_id(0) < nu_ref[0]

    @pl.when(used & (pl.program_id(1) == 0))
    def _():
        half = xs_ref.shape[1]
        first, second = _unpack_bf16_pair(xs_ref[...])
        xb_ref[:, 0:half] = first.astype(BF16)
        xb_ref[:, half:2 * half] = second.astype(BF16)

    @pl.when(used)
    def _():
        xb = xb_ref[...]
        gl = jnp.minimum(_dot(xb, wg_ref[0, 0].astype(BF16)) + bg_ref[0, 0], SWIGLU_LIMIT)
        lin = jnp.clip(_dot(xb, wu_ref[0, 0].astype(BF16)) + bu_ref[0, 0], -SWIGLU_LIMIT, SWIGLU_LIMIT)
        h_ref[...] = (gl * jax.nn.sigmoid(SWIGLU_ALPHA * gl) * (lin + 1.0)).astype(BF16)

    @pl.when(jnp.logical_not(used))
    def _():
        h_ref[...] = jnp.zeros(h_ref.shape, BF16)


def _ffn_down_kernel(be_ref, nu_ref, h_ref, wd_ref, bd_ref, y_ref):
    del be_ref
    used = pl.program_id(1) < nu_ref[0]

    @pl.when(used)
    def _():
        y = _dot(h_ref[...], wd_ref[0, 0].astype(BF16)) + bd_ref[0, 0]
        half = y.shape[1] // 2
        y_ref[...] = _pack_bf16_pair(y[:, 0:half], y[:, half:2 * half])

    @pl.when(jnp.logical_not(used))
    def _():
        y_ref[...] = jnp.zeros(y_ref.shape, jnp.uint32)


def _expert_ffn(layer, block_expert, n_used, xs, w_gate, b_gate, w_up, b_up, w_down, b_down):
    n_buf = xs.shape[0]
    depth, n_exp, d, f = w_gate.shape
    tm, tf, td = EXPERT_TM, EXPERT_TF, EXPERT_TD
    n_blk = n_buf // tm

    nf = f // tf

    def up_rows(i, j, be, nu):
        return (jnp.minimum(i, nu[0] - 1), 0)

    def up_cols(i, j, be, nu):
        return (layer, be[i], 0, jnp.where(i < nu[0], j, nf - 1))

    def rows(j, i, be, nu):
        return (jnp.minimum(i, nu[0] - 1), 0)

    def cols(j, i, be, nu):
        return (layer, be[i], 0, j)

    def out_block(j, i, be, nu):
        return (i, j)

    hidden = pl.pallas_call(
        _ffn_up_kernel,
        grid_spec=pltpu.PrefetchScalarGridSpec(
            num_scalar_prefetch=2,
            grid=(n_blk, nf),
            in_specs=[
                pl.BlockSpec((tm, d // 2), up_rows),
                pl.BlockSpec((1, 1, d, tf), up_cols),
                pl.BlockSpec((1, 1, 1, tf), up_cols),
                pl.BlockSpec((1, 1, d, tf), up_cols),
                pl.BlockSpec((1, 1, 1, tf), up_cols),
            ],
            out_specs=pl.BlockSpec((tm, tf), lambda i, j, be, nu: (i, j)),
            scratch_shapes=[pltpu.VMEM((tm, d), BF16)],
        ),
        out_shape=jax.ShapeDtypeStruct((n_buf, f), BF16),
        compiler_params=_params("arbitrary", "arbitrary"),
        name="moe_ffn_up",
    )(block_expert, n_used, xs, w_gate, b_gate.reshape(depth, n_exp, 1, f), w_up, b_up.reshape(depth, n_exp, 1, f))
    return pl.pallas_call(
        _ffn_down_kernel,
        grid_spec=pltpu.PrefetchScalarGridSpec(
            num_scalar_prefetch=2,
            grid=(d // td, n_blk),
            in_specs=[
                pl.BlockSpec((tm, f), rows),
                pl.BlockSpec((1, 1, f, td), cols),
                pl.BlockSpec((1, 1, 1, td), cols),
            ],
            out_specs=pl.BlockSpec((tm, td // 2), out_block),
        ),
        out_shape=jax.ShapeDtypeStruct((n_buf, d // 2), jnp.uint32),
        compiler_params=_params("arbitrary", "arbitrary"),
        name="moe_ffn_down",
    )(block_expert, n_used, hidden, w_down, b_down.reshape(depth, n_exp, 1, d))


def _combine_kernel(dest_ref, wt_ref, x1_ref, g2_ref, fg_ref, y_ref, o_ref, rows_ref, sem, *, final_norm):
    tm = x1_ref.shape[1]

    def issue(t, carry):
        for k in range(TOP_K):
            _row_copy(y_ref, dest_ref[k, t], rows_ref.at[k], t, sem).start()
        return carry

    lax.fori_loop(0, tm, issue, 0)

    wt = jnp.concatenate([wt_ref[...], jnp.zeros((tm - SUBLANES, tm), F32)], axis=0).T

    def drain(t, carry):
        for k in range(TOP_K):
            _row_copy(y_ref, dest_ref[k, t], rows_ref.at[k], t, sem).wait()
        return carry

    lax.fori_loop(0, tm, drain, 0)

    acc_first, acc_second = None, None
    for k in range(TOP_K):
        first, second = _unpack_bf16_pair(rows_ref[k])
        w = wt[:, k:k + 1]
        acc_first = w * first if k == 0 else acc_first + w * first
        acc_second = w * second if k == 0 else acc_second + w * second
    half = EXPERT_TD // 2
    pieces = []
    for j in range(acc_first.shape[1] // half):
        pieces += [acc_first[:, j * half:(j + 1) * half], acc_second[:, j * half:(j + 1) * half]]
    out = x1_ref[0] + g2_ref[0] * jnp.concatenate(pieces, axis=1)
    if final_norm:
        out = _rmsnorm(out, fg_ref[...])
    o_ref[0] = out


def _combine(dest_t, wt_t, x1, gate2, final_g, y_rows, final_norm):
    b, s, d = x1.shape
    tm = COMBINE_TM
    flat = lambda bi, m: (0, bi * (s // tm) + m)
    return pl.pallas_call(
        functools.partial(_combine_kernel, final_norm=final_norm),
        grid=(b, s // tm),
        in_specs=[
            pl.BlockSpec((TOP_K, tm), flat, memory_space=pltpu.SMEM),
            pl.BlockSpec((SUBLANES, tm), flat),
            pl.BlockSpec((1, tm, d), lambda bi, m: (bi, m, 0)),
            pl.BlockSpec((1, 1, d), lambda bi, m: (bi, 0, 0)),
            pl.BlockSpec((1, d), lambda bi, m: (0, 0)),
            pl.BlockSpec(memory_space=pl.ANY),
        ],
        out_specs=pl.BlockSpec((1, tm, d), lambda bi, m: (bi, m, 0)),
        out_shape=jax.ShapeDtypeStruct((b, s, d), F32),
        scratch_shapes=[pltpu.VMEM((TOP_K, tm, d // 2), jnp.uint32), pltpu.SemaphoreType.DMA(())],
        compiler_params=_params("arbitrary", "arbitrary"),
        name="moe_combine",
    )(dest_t, wt_t, x1, gate2, final_g, y_rows)


def _segment_layout(counts, n_blk):
    padded = (counts + EXPERT_TM - 1) // EXPERT_TM * EXPERT_TM
    ends = jnp.cumsum(padded)
    starts = ends - padded
    n_used = ends[-1] // EXPERT_TM
    blk = jnp.minimum(jnp.arange(n_blk, dtype=I32), n_used - 1) * EXPERT_TM
    block_expert = jnp.minimum(jnp.searchsorted(ends, blk, side="right"), counts.shape[0] - 1)
    return starts.astype(I32), ends.astype(I32), block_expert.astype(I32), n_used.reshape(1).astype(I32)


def kernel(x, c, norm1_g, norm2_g, w_ada, b_ada, w_in, pool_w, pool_scale, cmp_pos_k, cmp_w1_k, cmp_b1_k, cmp_w2_k, cmp_b2_k, cmp_pos_v, cmp_w1_v, cmp_b1_v, cmp_w2_v, cmp_b2_v, w_out, w_router, b_router, w_gate, b_gate, w_up, b_up, w_down, b_down, final_g):
    b, s, d = x.shape
    depth = w_ada.shape[0]
    n_exp = w_router.shape[-1]
    t_all = b * s
    assert s % PROJ_TM == 0 and s % SLC_CHUNK == 0 and t_all % DISPATCH_TM == 0
    n_main = POOL_WIDTH + NSA_WIDTH + 6 * KV_WIDTH
    assert w_in.shape[-1] == n_main + N_GATE_COLS

    c_pad = jnp.pad(c, ((0, SUBLANES - b), (0, 0)))
    mod = _modulation(c_pad, w_ada, b_ada)[:, :b]
    tri = (jnp.arange(PROJ_TM)[:, None] < jnp.arange(PROJ_TM)[None, :]).astype(BF16)
    n_buf = (t_all * TOP_K + n_exp * (EXPERT_TM - 1)) // EXPERT_TM * EXPERT_TM
    n_blk = n_buf // EXPERT_TM

    for l in range(depth):
        sh1, sc1, g1, sh2, sc2, g2 = [m.reshape(b, 1, d) for m in jnp.split(mod[l], 6, axis=-1)]
        w_main = w_in[l, :, :n_main].astype(BF16)
        w_gates = jnp.pad(w_in[l, :, n_main:], ((0, 0), (0, LANES - N_GATE_COLS))).astype(BF16)
        u_pool, qT, kvc, ksl, vslT, kwin, vwinT, gT = _input_projection(
            x, norm1_g[l].reshape(1, d), sc1, sh1, w_main, w_gates)
        y_pool = _pool_mixer(u_pool, pool_w[l].astype(BF16), pool_scale[l].reshape(1, POOL_WIDTH))
        kc, kcT = _compress(
            kvc,
            jnp.stack([cmp_pos_k[l], cmp_pos_v[l]]).reshape(2, 1, CMP_BLOCK * HEAD_DIM),
            jnp.stack([cmp_w1_k[l], cmp_w1_v[l]]).astype(BF16),
            jnp.stack([cmp_b1_k[l], cmp_b1_v[l]])[:, None, :],
            jnp.stack([cmp_w2_k[l], cmp_w2_v[l]]).astype(BF16),
            jnp.stack([cmp_b2_k[l], cmp_b2_v[l]])[:, None, :])
        y_nsa = _nsa(qT, kc, kcT, ksl, vslT, kwin, vwinT, gT)
        x1, h2, idx_t, wt_t, rank_t, counts = _output_projection(
            y_pool, y_nsa, w_out[l].astype(BF16), x, g1, norm2_g[l].reshape(1, d), sc2, sh2,
            w_router[l].T.astype(BF16), b_router[l].reshape(n_exp, 1), tri)
        seg_starts, seg_ends, block_expert, n_used = _segment_layout(counts[:, 0], n_blk)
        dest_t = _destinations(seg_starts, idx_t, rank_t)
        xs = _dispatch(dest_t, seg_starts, seg_ends, h2.reshape(t_all, d // 2), n_buf)
        y_rows = _expert_ffn(l, block_expert, n_used, xs, w_gate, b_gate, w_up, b_up, w_down, b_down)
        x = _combine(dest_t, wt_t, x1, g2, final_g.reshape(1, d), y_rows, final_norm=(l == depth - 1))
    return x
```
